```python
import jax
import jax.numpy as jnp
from jax import lax
import numpy as np

D_MODEL = 1024
BATCH = 1
SEQ = 16384
DEPTH = 4
DEC_BATCH = 32
DEC_SEQ = 64
PAST_LEN = 2048

CHUNK = 64
N_MIXERS = 3
N_LAYERS_A = (DEPTH + 2) // 3
N_LAYERS_B = (DEPTH + 1) // 3
N_LAYERS_C = DEPTH // 3
RMS_EPS = 1e-6
D_FF = (8 * D_MODEL + 3 * 256 - 1) // (3 * 256) * 256

A_HEAD = 64
A_HEADS = D_MODEL // A_HEAD
A_DECAY_LORA = 64
A_AAA_LORA = 64
A_GATE_LORA = 128
A_LNX_EPS = 64e-5

B_HEADS = 4
B_DK = D_MODEL // (2 * B_HEADS)
B_DV = D_MODEL // B_HEADS
B_GATE_LORA = 16
B_GATE_TAU = 16.0

C_HEADS = 8
C_DK = D_MODEL // C_HEADS
C_DV = D_MODEL // C_HEADS
C_CONV = 4
C_QKV = C_HEADS * (2 * C_DK + C_DV)

kernel_name = "hybrid_rwkv7_gla_gdn_stream_step"


def rms_norm(x, g):
    xf = x.astype(jnp.float32)
    y = xf * lax.rsqrt(jnp.mean(xf * xf, axis=-1, keepdims=True) + RMS_EPS)
    return (y * g.astype(jnp.float32)).astype(x.dtype)


def gated_head_norm(o, gain, gate):
    on = o * lax.rsqrt(jnp.mean(o * o, axis=-1, keepdims=True) + RMS_EPS) * gain
    return on.reshape(gate.shape) * jax.nn.silu(gate)


def swiglu_ffn(x, w_in, w_out):
    gate, up = jnp.split(x @ w_in, 2, axis=-1)
    return (jax.nn.silu(gate) * up) @ w_out


def rwkv7_mix(x, shift_prev, wkv_prev, mu, w0, w1, w2, a0, a1, a2, g1, g2,
              k_k, k_a, r_k, w_rkv, w_o, lnx_w, lnx_b):
    B, T, D = x.shape
    H, N = A_HEADS, A_HEAD
    xf = x.astype(jnp.float32)
    x_prev = jnp.concatenate([shift_prev.astype(jnp.float32)[:, None, :], xf[:, :-1]], axis=1)
    xx = x_prev - xf
    xr, xw, xk, xv, xa, xg = (xf + xx * mu[i] for i in range(6))
    r = xr @ w_rkv[0]
    k = xk @ w_rkv[1]
    v = xv @ w_rkv[2]
    log_rate = -jax.nn.softplus(-(w0 + jnp.tanh(xw @ w1) @ w2)) - 0.5
    decay = jnp.exp(-jnp.exp(log_rate))
    a = jax.nn.sigmoid(a0 + (xa @ a1) @ a2)
    g = jax.nn.sigmoid(xg @ g1) @ g2
    heads = lambda t: t.reshape(B, T, H, N)
    kk = heads(k * k_k)
    kk = kk * lax.rsqrt(jnp.maximum(jnp.sum(kk * kk, axis=-1, keepdims=True), 1e-24))
    k_mod = heads(k * (1.0 + (a - 1.0) * k_a))
    r_h, v_h, w_h, a_h = heads(r), heads(v), heads(decay), heads(a)

    def step(S, inp):
        r_t, w_t, k_t, v_t, kk_t, a_t = inp
        S = (S * w_t[:, :, None, :]
             - jnp.einsum("bhvk,bhk->bhv", S, kk_t)[..., None] * (kk_t * a_t)[:, :, None, :]
             + v_t[..., None] * k_t[:, :, None, :])
        return S, jnp.einsum("bhvk,bhk->bhv", S, r_t)

    xs = tuple(jnp.moveaxis(t, 1, 0) for t in (r_h, w_h, k_mod, v_h, kk, a_h))
    S_fin, o = lax.scan(step, wkv_prev.astype(jnp.float32), xs)
    o = jnp.moveaxis(o, 0, 1)
    mean = jnp.mean(o, axis=-1, keepdims=True)
    var = jnp.mean(jnp.square(o - mean), axis=-1, keepdims=True)
    o = ((o - mean) * lax.rsqrt(var + A_LNX_EPS)).reshape(B, T, D) * lnx_w + lnx_b
    bonus = (jnp.sum(r_h * k_mod * r_k, axis=-1, keepdims=True) * v_h).reshape(B, T, D)
    y = ((o + bonus) * g) @ w_o
    return y.astype(x.dtype), x[:, -1], S_fin.astype(x.dtype)


def _blocks(t, C):
    B, T, H = t.shape[:3]
    return jnp.moveaxis(t.reshape(B, T // C, C, H, *t.shape[3:]), 3, 2)


def gla_chunked(q, k, v, gk, S0, C):
    B, T, H, DV = v.shape
    q, k, v, gk = (_blocks(t, C) for t in (q, k, v, gk))
    b = jnp.cumsum(gk, axis=-2)
    b_last = b[..., -1:, :]
    q_t = q * jnp.exp(b)
    k_t = k * jnp.exp(-b)
    k_hat = k * jnp.exp(b_last - b)
    incl = jnp.tril(jnp.ones((C, C), dtype=bool))
    A = jnp.where(incl, jnp.einsum("bnhik,bnhjk->bnhij", q_t, k_t), 0.0)
    o_intra = jnp.einsum("bnhij,bnhjv->bnhiv", A, v)
    kv_chunk = jnp.einsum("bnhjk,bnhjv->bnhkv", k_hat, v)
    chunk_decay = jnp.exp(b_last[..., 0, :])

    def step(S, inp):
        q_c, d_c, kv_c = inp
        return S * d_c[..., None] + kv_c, jnp.einsum("bhik,bhkv->bhiv", q_c, S)

    xs = tuple(jnp.moveaxis(t, 1, 0) for t in (q_t, chunk_decay, kv_chunk))
    S_fin, o_inter = lax.scan(step, S0, xs)
    o = o_intra + jnp.moveaxis(o_inter, 0, 1)
    return jnp.moveaxis(o, 2, 3).reshape(B, T, H, DV), S_fin


def gla_mix(x, kv_prev, w_in, w_a1, w_a2, b_a, onorm, w_o):
    B, T, _ = x.shape
    H, DK, DV = B_HEADS, B_DK, B_DV
    xf = x.astype(jnp.float32)
    q, k, v, gate = jnp.split(xf @ w_in, [H * DK, 2 * H * DK, 2 * H * DK + H * DV], axis=-1)
    gk = jax.nn.log_sigmoid((xf @ w_a1) @ w_a2 + b_a) / B_GATE_TAU
    o, S = gla_chunked(q.reshape(B, T, H, DK) * DK ** -0.5, k.reshape(B, T, H, DK),
                       v.reshape(B, T, H, DV), gk.reshape(B, T, H, DK),
                       kv_prev.astype(jnp.float32), min(CHUNK, T))
    y = gated_head_norm(o, onorm, gate) @ w_o
    return y.astype(x.dtype), S.astype(x.dtype)


def gdn_chunked(q, k, v, g, beta, S0, C):
    B, T, H, DV = v.shape
    q, k, v, g, beta = (_blocks(t, C) for t in (q, k, v, g, beta))
    gc = jnp.cumsum(g, axis=-1)
    incl = jnp.tril(jnp.ones((C, C), dtype=bool))
    strict = jnp.tril(jnp.ones((C, C), dtype=bool), -1)
    decay = jnp.where(incl, jnp.exp(jnp.where(incl, gc[..., :, None] - gc[..., None, :], 0.0)), 0.0)
    k_beta = k * beta[..., None]
    L = jnp.where(strict, jnp.einsum("bnhik,bnhjk->bnhij", k_beta, k) * decay, 0.0)
    rhs = jnp.concatenate([v * beta[..., None], k_beta * jnp.exp(gc)[..., None]], axis=-1)
    sol = lax.linalg.triangular_solve(jnp.eye(C, dtype=L.dtype) + L, rhs,
                                      left_side=True, lower=True, unit_diagonal=True)
    u, w = sol[..., :DV], sol[..., DV:]
    attn_qk = jnp.einsum("bnhik,bnhjk->bnhij", q, k) * decay
    q_dec = q * jnp.exp(gc)[..., None]
    k_dec = k * jnp.exp(gc[..., -1:] - gc)[..., None]
    chunk_decay = jnp.exp(gc[..., -1])

    def step(S, inp):
        q_c, k_c, u_c, w_c, a_c, d_c = inp
        v_new = u_c - jnp.einsum("bhik,bhkv->bhiv", w_c, S)
        o_c = jnp.einsum("bhik,bhkv->bhiv", q_c, S) + jnp.einsum("bhij,bhjv->bhiv", a_c, v_new)
        S = S * d_c[..., None, None] + jnp.einsum("bhjk,bhjv->bhkv", k_c, v_new)
        return S, o_c

    xs = tuple(jnp.moveaxis(t, 1, 0) for t in (q_dec, k_dec, u, w, attn_qk, chunk_decay))
    S_fin, o = lax.scan(step, S0, xs)
    o = jnp.moveaxis(o, 0, 1)
    return jnp.moveaxis(o, 2, 3).reshape(B, T, H, DV), S_fin


def gdn_mix(x, conv_prev, kv_prev, w_in, conv_w, a_log, dt_bias, onorm, w_o):
    B, T, _ = x.shape
    H, DK, DV = C_HEADS, C_DK, C_DV
    xf = x.astype(jnp.float32)
    qkv, z, b_raw, a_raw = jnp.split(xf @ w_in, [C_QKV, C_QKV + H * DV, C_QKV + H * DV + H], axis=-1)
    padded = jnp.concatenate([conv_prev.astype(jnp.float32), qkv], axis=1)
    conv = padded[:, 0:T] * conv_w[0]
    for i in range(1, C_CONV):
        conv = conv + padded[:, i:i + T] * conv_w[i]
    q, k, v = jnp.split(jax.nn.silu(conv), [H * DK, 2 * H * DK], axis=-1)
    l2 = lambda t: t * lax.rsqrt(jnp.sum(t * t, axis=-1, keepdims=True) + 1e-6)
    q = l2(q.reshape(B, T, H, DK)) * DK ** -0.5
    k = l2(k.reshape(B, T, H, DK))
    v = v.reshape(B, T, H, DV)
    beta = jax.nn.sigmoid(b_raw)
    g = -jnp.exp(a_log) * jax.nn.softplus(a_raw + dt_bias)
    o, S = gdn_chunked(q, k, v, g, beta, kv_prev.astype(jnp.float32), min(CHUNK, T))
    y = gated_head_norm(o, onorm, z) @ w_o
    return y.astype(x.dtype), padded[:, -(C_CONV - 1):].astype(x.dtype), S.astype(x.dtype)


def trunk(x, a_shift, a_wkv, b_kv, c_conv, c_kv, norm_mix, norm_ffn, norm_final,
          ffn_w_in, ffn_w_out, pa, pb, pc):
    out_a_shift, out_a_wkv, out_b_kv, out_c_conv, out_c_kv = [], [], [], [], []
    for i in range(DEPTH):
        j = i // N_MIXERS
        h = rms_norm(x, norm_mix[i])
        if i % N_MIXERS == 0:
            y, s_shift, s_wkv = rwkv7_mix(h, a_shift[j], a_wkv[j], *(p[j] for p in pa))
            out_a_shift.append(s_shift)
            out_a_wkv.append(s_wkv)
        elif i % N_MIXERS == 1:
            y, s_kv = gla_mix(h, b_kv[j], *(p[j] for p in pb))
            out_b_kv.append(s_kv)
        else:
            y, s_conv, s_kv = gdn_mix(h, c_conv[j], c_kv[j], *(p[j] for p in pc))
            out_c_conv.append(s_conv)
            out_c_kv.append(s_kv)
        x = x + y
        x = x + swiglu_ffn(rms_norm(x, norm_ffn[i]), ffn_w_in[i], ffn_w_out[i])
    return (rms_norm(x, norm_final), jnp.stack(out_a_shift), jnp.stack(out_a_wkv),
            jnp.stack(out_b_kv), jnp.stack(out_c_conv), jnp.stack(out_c_kv))


def setup_inputs(seed: int = 0) -> dict:
    key = jax.random.key(seed)
    keys = iter(jax.random.split(key, 64))

    def nrm(shape, scale):
        return jax.random.normal(next(keys), shape, jnp.float32) * scale

    def unif(shape, lo, hi):
        return jax.random.uniform(next(keys), shape, jnp.float32, lo, hi)

    D, F = D_MODEL, D_FF
    NA, NB, NCL = N_LAYERS_A, N_LAYERS_B, N_LAYERS_C
    return {
        "x_prompt": nrm((BATCH, SEQ, D), 1.0),
        "x_sample": nrm((DEC_BATCH, DEC_SEQ, D), 1.0),
        "state_a_shift": nrm((NA, DEC_BATCH, D), 1.0),
        "state_a_wkv": nrm((NA, DEC_BATCH, A_HEADS, A_HEAD, A_HEAD), 0.5),
        "state_b_kv": nrm((NB, DEC_BATCH, B_HEADS, B_DK, B_DV), 1.0),
        "state_c_conv": nrm((NCL, DEC_BATCH, C_CONV - 1, C_QKV), 1.0),
        "state_c_kv": nrm((NCL, DEC_BATCH, C_HEADS, C_DK, C_DV), 0.1),
        "norm_mix": 1.0 + nrm((DEPTH, D), 0.02),
        "norm_ffn": 1.0 + nrm((DEPTH, D), 0.02),
        "norm_final": 1.0 + nrm((D,), 0.02),
        "ffn_w_in": nrm((DEPTH, D, 2 * F), D ** -0.5),
        "ffn_w_out": nrm((DEPTH, F, D), F ** -0.5),
        "a_mu": unif((NA, 6, D), 0.0, 1.0),
        "a_w0": unif((NA, D), -6.0, -1.0),
        "a_w1": nrm((NA, D, A_DECAY_LORA), D ** -0.5),
        "a_w2": nrm((NA, A_DECAY_LORA, D), 0.1 * A_DECAY_LORA ** -0.5),
        "a_a0": nrm((NA, D), 0.1),
        "a_a1": nrm((NA, D, A_AAA_LORA), D ** -0.5),
        "a_a2": nrm((NA, A_AAA_LORA, D), 0.1 * A_AAA_LORA ** -0.5),
        "a_g1": nrm((NA, D, A_GATE_LORA), D ** -0.5),
        "a_g2": nrm((NA, A_GATE_LORA, D), A_GATE_LORA ** -0.5),
        "a_k_k": 0.85 + nrm((NA, D), 0.05),
        "a_k_a": 1.0 + nrm((NA, D), 0.05),
        "a_r_k": nrm((NA, A_HEADS, A_HEAD), 0.1),
        "a_w_rkv": nrm((NA, 3, D, D), D ** -0.5),
        "a_w_o": nrm((NA, D, D), D ** -0.5),
        "a_lnx_w": 1.0 + nrm((NA, D), 0.02),
        "a_lnx_b": nrm((NA, D), 0.02),
        "b_w_in": nrm((NB, D, 2 * B_HEADS * B_DK + 2 * B_HEADS * B_DV), D ** -0.5),
        "b_w_a1": nrm((NB, D, B_GATE_LORA), D ** -0.5),
        "b_w_a2": nrm((NB, B_GATE_LORA, B_HEADS * B_DK), B_GATE_LORA ** -0.5),
        "b_b_a": nrm((NB, B_HEADS * B_DK), 0.1),
        "b_onorm": 1.0 + nrm((NB, B_DV), 0.02),
        "b_w_o": nrm((NB, B_HEADS * B_DV, D), (B_HEADS * B_DV) ** -0.5),
        "c_w_in": nrm((NCL, D, C_QKV + C_HEADS * C_DV + 2 * C_HEADS), D ** -0.5),
        "c_conv_w": nrm((NCL, C_CONV, C_QKV), C_CONV ** -0.5),
        "c_a_log": jnp.log(unif((NCL, C_HEADS), 1.0, 16.0)),
        "c_dt_bias": jnp.log(jnp.expm1(unif((NCL, C_HEADS), 0.001, 0.1))),
        "c_onorm": 1.0 + nrm((NCL, C_DV), 0.02),
        "c_w_o": nrm((NCL, C_HEADS * C_DV, D), (C_HEADS * C_DV) ** -0.5),
    }


def reference(x_prompt, x_sample, state_a_shift, state_a_wkv, state_b_kv, state_c_conv, state_c_kv,
              norm_mix, norm_ffn, norm_final, ffn_w_in, ffn_w_out,
              a_mu, a_w0, a_w1, a_w2, a_a0, a_a1, a_a2, a_g1, a_g2, a_k_k, a_k_a, a_r_k,
              a_w_rkv, a_w_o, a_lnx_w, a_lnx_b,
              b_w_in, b_w_a1, b_w_a2, b_b_a, b_onorm, b_w_o,
              c_w_in, c_conv_w, c_a_log, c_dt_bias, c_onorm, c_w_o):
    pa = (a_mu, a_w0, a_w1, a_w2, a_a0, a_a1, a_a2, a_g1, a_g2, a_k_k, a_k_a, a_r_k,
          a_w_rkv, a_w_o, a_lnx_w, a_lnx_b)
    pb = (b_w_in, b_w_a1, b_w_a2, b_b_a, b_onorm, b_w_o)
    pc = (c_w_in, c_conv_w, c_a_log, c_dt_bias, c_onorm, c_w_o)
    bp, dt = x_prompt.shape[0], x_prompt.dtype
    zero_a_shift = jnp.zeros((N_LAYERS_A, bp, D_MODEL), dt)
    zero_a_wkv = jnp.zeros((N_LAYERS_A, bp, A_HEADS, A_HEAD, A_HEAD), dt)
    zero_b_kv = jnp.zeros((N_LAYERS_B, bp, B_HEADS, B_DK, B_DV), dt)
    zero_c_conv = jnp.zeros((N_LAYERS_C, bp, C_CONV - 1, C_QKV), dt)
    zero_c_kv = jnp.zeros((N_LAYERS_C, bp, C_HEADS, C_DK, C_DV), dt)
    y_prompt, a_shift_p, a_wkv_p, b_kv_p, c_conv_p, c_kv_p = trunk(
        x_prompt, zero_a_shift, zero_a_wkv, zero_b_kv, zero_c_conv, zero_c_kv,
        norm_mix, norm_ffn, norm_final, ffn_w_in, ffn_w_out, pa, pb, pc)
    y_sample, a_shift_s, a_wkv_s, b_kv_s, c_conv_s, c_kv_s = trunk(
        x_sample, state_a_shift, state_a_wkv, state_b_kv, state_c_conv, state_c_kv,
        norm_mix, norm_ffn, norm_final, ffn_w_in, ffn_w_out, pa, pb, pc)
    return (y_prompt, y_sample, a_shift_p, a_wkv_p, b_kv_p, c_conv_p, c_kv_p,
            a_shift_s, a_wkv_s, b_kv_s, c_conv_s, c_kv_s)
```

```python
import functools

import jax
import jax.numpy as jnp
from jax import lax
from jax.experimental import pallas as pl
from jax.experimental.pallas import tpu as pltpu

F32 = jnp.float32
BF16 = jnp.bfloat16

D_MODEL = 1024
CHUNK = 64
RMS_EPS = 1e-6
D_FF = 2816
FFN_COLS = 256

A_HEADS = 16
A_HEAD = 64
A_PAIRS = A_HEADS // 2
A_LNX_EPS = 64e-5

B_HEADS = 4
B_DK = 128
B_DV = 256
B_GATE_TAU = 16.0
B_LORA_PAD = 128

C_HEADS = 8
C_DK = 128
C_DV = 128
C_CONV = 4
C_QKV = C_HEADS * (2 * C_DK + C_DV)
C_PROJ_PAD = C_QKV + C_HEADS * C_DV + 128

TOKEN_TILE = 512
VMEM_LIMIT = 56 * 1024 * 1024


def _mm(a, b):
    return jnp.dot(a.astype(BF16), b.astype(BF16), preferred_element_type=F32)


def _mm_nt(a, b):
    return lax.dot_general(a.astype(BF16), b.astype(BF16), (((1,), (1,)), ((), ())),
                           preferred_element_type=F32)


def _mm_tn(a, b):
    return lax.dot_general(a.astype(BF16), b.astype(BF16), (((0,), (0,)), ((), ())),
                           preferred_element_type=F32)


def _split3(x):
    hi = x.astype(BF16)
    r1 = x - hi.astype(F32)
    mid = r1.astype(BF16)
    lo = (r1 - mid.astype(F32)).astype(BF16)
    return hi, mid, lo


def _exact_left(m_bf16, x):
    hi, mid, lo = _split3(x)
    dot = lambda p: jnp.dot(m_bf16, p, preferred_element_type=F32)
    return dot(hi) + dot(mid) + dot(lo)


def _exact_tn(x, m_bf16):
    hi, mid, lo = _split3(x)
    dot = lambda p: lax.dot_general(p, m_bf16, (((0,), (0,)), ((), ())), preferred_element_type=F32)
    return dot(hi) + dot(mid) + dot(lo)


def _rms(x, gain):
    return x * lax.rsqrt(jnp.mean(x * x, axis=-1, keepdims=True) + RMS_EPS) * gain


def _softplus(x):
    return jnp.maximum(x, 0.0) + jnp.log1p(jnp.exp(-jnp.abs(x)))


def _sigmoid(x):
    return 1.0 / (1.0 + jnp.exp(-x))


def _silu(x):
    return x * _sigmoid(x)


def _iota(shape, dim):
    return lax.broadcasted_iota(jnp.int32, shape, dim)


def _neumann_inverse(n, steps):
    size = n.shape[0]
    eye = (_iota((size, size), 0) == _iota((size, size), 1)).astype(F32)
    x = eye + n
    m = n
    for _ in range(steps):
        m = _mm(m, m)
        x = x + _mm(x, m)
    return x


def _seq_flags(c, npch, cps, cpd):
    start = jnp.where(c < npch, c % cps == 0, (c - npch) % cpd == 0)
    end = jnp.where(c < npch, c % cps == cps - 1, (c - npch) % cpd == cpd - 1)
    return start, end


def _seq_index(c, npch, cps, cpd, bp):
    return jnp.where(c < npch, c // cps, bp + (c - npch) // cpd)


def _proj_body(x_ref, gain_ref, w_ref, o_ref, h_ref):
    @pl.when(pl.program_id(1) == 0)
    def _():
        h_ref[...] = _rms(x_ref[...], gain_ref[...]).astype(BF16)

    o_ref[...] = jnp.dot(h_ref[...], w_ref[...], preferred_element_type=F32)


def _norm_proj(x, gain, w, tn):
    nt, n = x.shape[0], w.shape[1]
    tm = TOKEN_TILE
    return pl.pallas_call(
        _proj_body,
        grid=(nt // tm, n // tn),
        in_specs=[pl.BlockSpec((tm, D_MODEL), lambda i, j: (i, 0)),
                  pl.BlockSpec((1, D_MODEL), lambda i, j: (0, 0)),
                  pl.BlockSpec((D_MODEL, tn), lambda i, j: (0, j))],
        out_specs=pl.BlockSpec((tm, tn), lambda i, j: (i, j)),
        out_shape=jax.ShapeDtypeStruct((nt, n), F32),
        scratch_shapes=[pltpu.VMEM((tm, D_MODEL), BF16)],
        compiler_params=pltpu.CompilerParams(
            dimension_semantics=("arbitrary", "arbitrary"), vmem_limit_bytes=VMEM_LIMIT),
        name="norm_proj",
    )(x, gain, w)


def _ffn_body(x_ref, y_ref, wo_ref, gain_ref, win_ref, wout_ref, fgain_ref, o_ref, of_ref, acc_ref, *, final):
    x1 = x_ref[...] + jnp.dot(y_ref[...].astype(BF16), wo_ref[...], preferred_element_type=F32)
    h = _rms(x1, gain_ref[...]).astype(BF16)
    acc_ref[...] = x1
    for c in range(D_FF // FFN_COLS):
        lo = c * FFN_COLS
        g = jnp.dot(h, win_ref[:, lo:lo + FFN_COLS], preferred_element_type=F32)
        u = jnp.dot(h, win_ref[:, D_FF + lo:D_FF + lo + FFN_COLS], preferred_element_type=F32)
        a = (_silu(g) * u).astype(BF16)
        acc_ref[...] += jnp.dot(a, wout_ref[lo:lo + FFN_COLS, :], preferred_element_type=F32)
    out = acc_ref[...]
    o_ref[...] = out
    if final:
        of_ref[...] = _rms(out, fgain_ref[...])
    else:
        of_ref[...] = jnp.zeros_like(of_ref)


def _out_ffn(x, y, wo, gain, win, wout, fgain, final):
    nt = x.shape[0]
    tm = TOKEN_TILE
    row = lambda i: (i, 0)
    fixed = lambda i: (0, 0)
    of_shape = (nt, D_MODEL) if final else (8, 128)
    of_spec = pl.BlockSpec((tm, D_MODEL), row) if final else pl.BlockSpec((8, 128), fixed)
    return pl.pallas_call(
        functools.partial(_ffn_body, final=final),
        grid=(nt // tm,),
        in_specs=[pl.BlockSpec((tm, D_MODEL), row),
                  pl.BlockSpec((tm, D_MODEL), row),
                  pl.BlockSpec((D_MODEL, D_MODEL), fixed),
                  pl.BlockSpec((1, D_MODEL), fixed),
                  pl.BlockSpec((D_MODEL, 2 * D_FF), fixed),
                  pl.BlockSpec((D_FF, D_MODEL), fixed),
                  pl.BlockSpec((1, D_MODEL), fixed)],
        out_specs=[pl.BlockSpec((tm, D_MODEL), row), of_spec],
        out_shape=[jax.ShapeDtypeStruct((nt, D_MODEL), F32), jax.ShapeDtypeStruct(of_shape, F32)],
        scratch_shapes=[pltpu.VMEM((tm, D_MODEL), F32)],
        compiler_params=pltpu.CompilerParams(
            dimension_semantics=("arbitrary",), vmem_limit_bytes=VMEM_LIMIT),
        name="out_ffn",
    )(x, y, wo, gain, win, wout, fgain)


def _a_proj_body(x_ref, xp_ref, sh_ref, gain_ref, mu_ref, w0_ref, a0_ref,
                 wr_ref, wk_ref, wv_ref, w1_ref, a1_ref, g1_ref, w2_ref, a2_ref, g2_ref,
                 r_ref, k_ref, v_ref, lw_ref, al_ref, gg_ref, hl_ref, *, tm, npch, cps, cpd):
    i = pl.program_id(0)
    gain = gain_ref[...]
    h = _rms(x_ref[...], gain)
    prev_tile_last = _rms(xp_ref[...], gain)[7:8, :]
    row = _iota((CHUNK, 1), 0)
    pieces = []
    for c in range(tm // CHUNK):
        hc = h[c * CHUNK:(c + 1) * CHUNK]
        natural = prev_tile_last if c == 0 else h[c * CHUNK - 1:c * CHUNK]
        start, _ = _seq_flags(i * (tm // CHUNK) + c, npch, cps, cpd)
        first = jnp.where(start, sh_ref[c:c + 1, :], natural)
        pieces.append(jnp.where(row == 0, first, pltpu.roll(hc, 1, 0)))
        hl_ref[c:c + 1, :] = hc[CHUNK - 1:CHUNK]
    xx = jnp.concatenate(pieces, axis=0) - h
    mix = lambda j: h + xx * mu_ref[j:j + 1, :]
    r_ref[...] = _mm(mix(0), wr_ref[...])
    log_rate = -_softplus(-(w0_ref[...] + _mm(jnp.tanh(_mm(mix(1), w1_ref[...])), w2_ref[...]))) - 0.5
    lw_ref[...] = -jnp.exp(log_rate)
    k_ref[...] = _mm(mix(2), wk_ref[...])
    v_ref[...] = _mm(mix(3), wv_ref[...])
    al_ref[...] = _sigmoid(a0_ref[...] + _mm(_mm(mix(4), a1_ref[...]), a2_ref[...]))
    gg_ref[...] = _mm(_sigmoid(_mm(mix(5), g1_ref[...])), g2_ref[...])


def _a_proj(x, shift_rows, gain, mu, w0, a0, wr, wk, wv, w1, a1, g1, w2, a2, g2, npch, cps, cpd):
    nt = x.shape[0]
    tm = TOKEN_TILE
    row = lambda i: (i, 0)
    fixed = lambda i: (0, 0)
    full = lambda a: pl.BlockSpec(a.shape, fixed)
    tok = pl.BlockSpec((tm, D_MODEL), row)
    tok_shape = jax.ShapeDtypeStruct((nt, D_MODEL), F32)
    return pl.pallas_call(
        functools.partial(_a_proj_body, tm=tm, npch=npch, cps=cps, cpd=cpd),
        grid=(nt // tm,),
        in_specs=[tok,
                  pl.BlockSpec((8, D_MODEL), lambda i: (jnp.maximum(i * (tm // 8) - 1, 0), 0)),
                  pl.BlockSpec((tm // CHUNK, D_MODEL), row),
                  full(gain), full(mu), full(w0), full(a0), full(wr), full(wk), full(wv),
                  full(w1), full(a1), full(g1), full(w2), full(a2), full(g2)],
        out_specs=[tok] * 6 + [pl.BlockSpec((tm // CHUNK, D_MODEL), row)],
        out_shape=[tok_shape] * 6 + [jax.ShapeDtypeStruct((nt // CHUNK, D_MODEL), F32)],
        compiler_params=pltpu.CompilerParams(
            dimension_semantics=("arbitrary",), vmem_limit_bytes=VMEM_LIMIT),
        name="rwkv_proj",
    )(x, x, shift_rows, gain, mu, w0, a0, wr, wk, wv, w1, a1, g1, w2, a2, g2)


def _a_chunk_body(r_ref, k_ref, v_ref, lw_ref, al_ref, gg_ref, kk_ref, ka_ref, rk_ref, lnw_ref, lnb_ref,
                  hin_ref, y_ref, hout_ref, hs_ref, *, npch, cps, cpd):
    c = pl.program_id(0)
    start, end = _seq_flags(c, npch, cps, cpd)

    @pl.when(start)
    def _():
        hs_ref[...] = hin_ref[0]

    lane = _iota((1, 128), 1)
    head0 = lane < A_HEAD
    ri = _iota((128, 128), 0) % CHUNK
    ci = _iota((128, 128), 1) % CHUNK
    strict = ri > ci
    incl = ri >= ci
    tri = (_iota((CHUNK, CHUNK), 0) >= _iota((CHUNK, CHUNK), 1)).astype(BF16)
    ones = jnp.ones((CHUNK, 128), BF16)

    def headsum(x):
        s0 = jnp.sum(jnp.where(head0, x, 0.0), axis=-1, keepdims=True)
        s1 = jnp.sum(jnp.where(head0, 0.0, x), axis=-1, keepdims=True)
        return jnp.where(head0, s0, s1)

    def expand(x):
        return jnp.concatenate([jnp.where(head0, x, 0.0), jnp.where(head0, 0.0, x)], axis=0)

    for p in range(A_PAIRS):
        sl = slice(p * 128, (p + 1) * 128)
        r = r_ref[:, sl]
        k = k_ref[:, sl]
        v = v_ref[:, sl]
        lw = lw_ref[:, sl]
        al = al_ref[:, sl]
        kk = k * kk_ref[:, sl]
        kk = kk * lax.rsqrt(jnp.maximum(headsum(kk * kk), 1e-24))
        kmod = k * (1.0 + (al - 1.0) * ka_ref[:, sl])

        gcum = _exact_left(tri, lw)
        glast = gcum[CHUNK - 1:CHUNK, :]
        dec_in = jnp.exp(gcum)
        dec_out = jnp.exp(-gcum)
        dec_end = jnp.exp(glast - gcum)
        ae = expand(-kk * jnp.exp(gcum - lw))
        be = expand(kk * al * dec_out)
        ke = expand(kmod * dec_out)
        re = expand(r * dec_in)
        ve = expand(v)
        bhe = expand(kk * al * dec_end)
        khe = expand(kmod * dec_end)

        n_ab = jnp.where(strict, _mm_nt(ae, be), 0.0)
        l_ak = jnp.where(strict, _mm_nt(ae, ke), 0.0)
        m_rb = jnp.where(incl, _mm_nt(re, be), 0.0)
        m_rk = jnp.where(incl, _mm_nt(re, ke), 0.0)
        tinv = _neumann_inverse(n_ab, 5)
        pm = _mm(tinv, ae)
        qm = _mm(tinv, _mm(l_ak, ve))
        rp = re + _mm(m_rb, pm)
        o_intra = _mm(m_rb, qm) + _mm(m_rk, ve)

        h0 = hs_ref[p]
        oe = _mm(rp, h0) + o_intra
        u = _mm(pm, h0) + qm
        wc = jnp.exp(_exact_tn(lw, ones))
        hs_ref[p] = wc * h0 + _mm_tn(bhe, u) + _mm_tn(khe, ve)

        o = oe[:CHUNK] + oe[CHUNK:]
        mean = headsum(o) * (1.0 / A_HEAD)
        cen = o - mean
        var = headsum(cen * cen) * (1.0 / A_HEAD)
        on = cen * lax.rsqrt(var + A_LNX_EPS) * lnw_ref[:, sl] + lnb_ref[:, sl]
        bonus = headsum(r * kmod * rk_ref[:, sl]) * v
        y_ref[:, sl] = (on + bonus) * gg_ref[:, sl]

    @pl.when(end)
    def _():
        hout_ref[0] = hs_ref[...]


def _a_chunk(r, k, v, lw, al, gg, k_k, k_a, r_k, lnx_w, lnx_b, h_in, npch, cps, cpd, bp):
    nt = r.shape[0]
    nseq = h_in.shape[0]
    tok = pl.BlockSpec((CHUNK, D_MODEL), lambda c: (c, 0))
    vec = pl.BlockSpec((1, D_MODEL), lambda c: (0, 0))
    st = pl.BlockSpec((1, A_PAIRS, 128, 128), lambda c: (_seq_index(c, npch, cps, cpd, bp), 0, 0, 0))
    return pl.pallas_call(
        functools.partial(_a_chunk_body, npch=npch, cps=cps, cpd=cpd),
        grid=(nt // CHUNK,),
        in_specs=[tok] * 6 + [vec] * 5 + [st],
        out_specs=[tok, st],
        out_shape=[jax.ShapeDtypeStruct((nt, D_MODEL), F32),
                   jax.ShapeDtypeStruct((nseq, A_PAIRS, 128, 128), F32)],
        scratch_shapes=[pltpu.VMEM((A_PAIRS, 128, 128), F32)],
        compiler_params=pltpu.CompilerParams(
            dimension_semantics=("arbitrary",), vmem_limit_bytes=VMEM_LIMIT),
        name="rwkv_chunk",
    )(r, k, v, lw, al, gg, k_k, k_a, r_k, lnx_w, lnx_b, h_in)


def _a_state_in(wkv, bp):
    b = wkv.shape[0]
    ht = jnp.swapaxes(wkv, -1, -2).reshape(b, A_PAIRS, 2, A_HEAD, A_HEAD)
    bd = jnp.einsum("bpikv,ij->bpikjv", ht, jnp.eye(2, dtype=wkv.dtype)).reshape(b, A_PAIRS, 128, 128)
    return jnp.concatenate([jnp.zeros((bp,) + bd.shape[1:], bd.dtype), bd], axis=0)


def _a_state_out(h):
    n = h.shape[0]
    h6 = h.reshape(n, A_PAIRS, 2, A_HEAD, 2, A_HEAD)
    diag = jnp.stack([h6[:, :, 0, :, 0, :], h6[:, :, 1, :, 1, :]], axis=2)
    return jnp.swapaxes(diag, -1, -2).reshape(n, A_HEADS, A_HEAD, A_HEAD)


def _head_rms_gate(o, gain, gate):
    return o * lax.rsqrt(jnp.mean(o * o, axis=-1, keepdims=True) + RMS_EPS) * gain * _silu(gate)


def _b_chunk_body(q_ref, k_ref, v_ref, gate_ref, lo_ref, wa2_ref, ba_ref, onorm_ref, sin_ref,
                  y_ref, sout_ref, ss_ref, *, npch, cps, cpd):
    c = pl.program_id(0)
    start, end = _seq_flags(c, npch, cps, cpd)

    @pl.when(start)
    def _():
        ss_ref[...] = sin_ref[0]

    incl = _iota((CHUNK, CHUNK), 0) >= _iota((CHUNK, CHUNK), 1)
    tri = incl.astype(BF16)
    ones = jnp.ones((CHUNK, B_DV), BF16)
    gk_all = -_softplus(-(_mm(lo_ref[...], wa2_ref[...]) + ba_ref[...])) * (1.0 / B_GATE_TAU)
    for h in range(B_HEADS):
        ks = slice(h * B_DK, (h + 1) * B_DK)
        vs = slice(h * B_DV, (h + 1) * B_DV)
        gk = gk_all[:, ks]
        k = k_ref[:, ks]
        v = v_ref[:, vs]
        b = _exact_left(tri, gk)
        blast = b[CHUNK - 1:CHUNK, :]
        qt = q_ref[:, ks] * (B_DK ** -0.5) * jnp.exp(b)
        kt = k * jnp.exp(-b)
        khat = k * jnp.exp(blast - b)
        a = jnp.where(incl, _mm_nt(qt, kt), 0.0)
        s0 = ss_ref[h]
        o = _mm(a, v) + _mm(qt, s0)
        ss_ref[h] = s0 * jnp.exp(_exact_tn(gk, ones)) + _mm_tn(khat, v)
        y_ref[:, vs] = _head_rms_gate(o, onorm_ref[...], gate_ref[:, vs])

    @pl.when(end)
    def _():
        sout_ref[0] = ss_ref[...]


def _b_chunk(proj, wa2, ba, onorm, s_in, npch, cps, cpd, bp):
    nt = proj.shape[0]
    nseq = s_in.shape[0]
    qk_w = B_HEADS * B_DK
    v_w = B_HEADS * B_DV
    st = pl.BlockSpec((1, B_HEADS, B_DK, B_DV), lambda c: (_seq_index(c, npch, cps, cpd, bp), 0, 0, 0))
    fixed = lambda c: (0, 0)
    return pl.pallas_call(
        functools.partial(_b_chunk_body, npch=npch, cps=cps, cpd=cpd),
        grid=(nt // CHUNK,),
        in_specs=[pl.BlockSpec((CHUNK, qk_w), lambda c: (c, 0)),
                  pl.BlockSpec((CHUNK, qk_w), lambda c: (c, 1)),
                  pl.BlockSpec((CHUNK, v_w), lambda c: (c, 1)),
                  pl.BlockSpec((CHUNK, v_w), lambda c: (c, 2)),
                  pl.BlockSpec((CHUNK, B_LORA_PAD), lambda c: (c, (2 * qk_w + 2 * v_w) // B_LORA_PAD)),
                  pl.BlockSpec(wa2.shape, fixed), pl.BlockSpec(ba.shape, fixed),
                  pl.BlockSpec(onorm.shape, fixed), st],
        out_specs=[pl.BlockSpec((CHUNK, v_w), lambda c: (c, 0)), st],
        out_shape=[jax.ShapeDtypeStruct((nt, v_w), F32),
                   jax.ShapeDtypeStruct((nseq, B_HEADS, B_DK, B_DV), F32)],
        scratch_shapes=[pltpu.VMEM((B_HEADS, B_DK, B_DV), F32)],
        compiler_params=pltpu.CompilerParams(
            dimension_semantics=("arbitrary",), vmem_limit_bytes=VMEM_LIMIT),
        name="gla_chunk",
    )(proj, proj, proj, proj, proj, wa2, ba, onorm, s_in)


def _c_chunk_body(qkv_ref, z_ref, ba_ref, cw_ref, alog_ref, dtb_ref, onorm_ref, cvin_ref, sin_ref,
                  y_ref, cvout_ref, sout_ref, ss_ref, pv_ref, *, npch, cps, cpd):
    c = pl.program_id(0)
    start, end = _seq_flags(c, npch, cps, cpd)

    @pl.when(start)
    def _():
        ss_ref[...] = sin_ref[0]
        pv_ref[...] = cvin_ref[0]

    x = qkv_ref[...]
    pv = pv_ref[...]
    row8 = _iota((8, 1), 0)
    conv = x * cw_ref[C_CONV - 1:C_CONV, :]
    for j in range(1, C_CONV):
        sh = pltpu.roll(x, j, 0)
        head = jnp.where(row8 < j, pltpu.roll(pv, j, 0), sh[:8])
        sh = jnp.concatenate([head, sh[8:]], axis=0)
        conv = conv + sh * cw_ref[C_CONV - 1 - j:C_CONV - j, :]
    pv_ref[...] = x[CHUNK - 8:]
    act = _silu(conv)

    ba = ba_ref[...]
    beta_all = _sigmoid(ba)
    g_all = -jnp.exp(alog_ref[...]) * _softplus(ba + dtb_ref[...])
    tri = (_iota((CHUNK, CHUNK), 0) >= _iota((CHUNK, CHUNK), 1)).astype(BF16)
    triu2 = (_iota((CHUNK, 128), 0) <= _iota((CHUNK, 128), 1) % CHUNK).astype(BF16)
    gc_col = _exact_left(tri, g_all)
    gc_row = _exact_tn(g_all, triu2)

    ri = _iota((128, 128), 0)
    ci = _iota((128, 128), 1)
    same = (ri // CHUNK) == (ci // CHUNK)
    strict = same & (ri % CHUNK > ci % CHUNK)
    incl = same & (ri % CHUNK >= ci % CHUNK)
    lane_lo = _iota((1, 128), 1) < CHUNK
    zeros = jnp.zeros((CHUNK, C_DK), F32)
    l2 = lambda t: t * lax.rsqrt(jnp.sum(t * t, axis=-1, keepdims=True) + 1e-6)

    for p in range(C_HEADS // 2):
        hs = (2 * p, 2 * p + 1)
        q, k, v, kb, col, beta = [], [], [], [], [], []
        for h in hs:
            q.append(l2(act[:, h * C_DK:(h + 1) * C_DK]) * (C_DK ** -0.5))
            k.append(l2(act[:, C_HEADS * C_DK + h * C_DK:C_HEADS * C_DK + (h + 1) * C_DK]))
            v.append(act[:, 2 * C_HEADS * C_DK + h * C_DV:2 * C_HEADS * C_DK + (h + 1) * C_DV])
            beta.append(beta_all[:, h:h + 1])
            kb.append(k[-1] * beta[-1])
            col.append(gc_col[:, C_HEADS + h:C_HEADS + h + 1])
        blockdiag = lambda t: jnp.concatenate(
            [jnp.concatenate([t[0], zeros], axis=1), jnp.concatenate([zeros, t[1]], axis=1)], axis=0)
        col_pair = jnp.concatenate(col, axis=0)
        row_pair = jnp.where(lane_lo, gc_row[C_HEADS + hs[0]:C_HEADS + hs[0] + 1, :],
                             gc_row[C_HEADS + hs[1]:C_HEADS + hs[1] + 1, :])
        decay = jnp.where(incl, jnp.exp(jnp.where(incl, col_pair - row_pair, 0.0)), 0.0)
        ke = blockdiag(k)
        l_pair = jnp.where(strict, _mm_nt(blockdiag(kb), ke), 0.0) * decay
        attn = _mm_nt(blockdiag(q), ke) * decay
        egc = [jnp.exp(cc) for cc in col]
        rhs = jnp.concatenate(
            [jnp.concatenate([v[i] * beta[i], kb[i] * egc[i]], axis=1) for i in range(2)], axis=0)
        sol = _mm(_neumann_inverse(-l_pair, 5), rhs)
        u_pair = sol[:, :C_DV]
        s0 = [ss_ref[h] for h in hs]
        ws = jnp.concatenate([_mm(sol[i * CHUNK:(i + 1) * CHUNK, C_DV:], s0[i]) for i in range(2)], axis=0)
        v_new = u_pair - ws
        o_intra = _mm(attn, v_new)
        for i, h in enumerate(hs):
            rows = slice(i * CHUNK, (i + 1) * CHUNK)
            glast = col[i][CHUNK - 1:CHUNK, :]
            o = _mm(q[i] * egc[i], s0[i]) + o_intra[rows]
            ss_ref[h] = s0[i] * jnp.exp(glast) + _mm_tn(k[i] * jnp.exp(glast - col[i]), v_new[rows])
            vs = slice(h * C_DV, (h + 1) * C_DV)
            y_ref[:, vs] = _head_rms_gate(o, onorm_ref[...], z_ref[:, vs])

    @pl.when(end)
    def _():
        sout_ref[0] = ss_ref[...]
        cvout_ref[0] = pv_ref[...]


def _c_chunk(proj, cw, alog, dtb, onorm, cv_in, s_in, npch, cps, cpd, bp):
    nt = proj.shape[0]
    nseq = s_in.shape[0]
    z_w = C_HEADS * C_DV
    seq = lambda c: _seq_index(c, npch, cps, cpd, bp)
    st = pl.BlockSpec((1, C_HEADS, C_DK, C_DV), lambda c: (seq(c), 0, 0, 0))
    cv = pl.BlockSpec((1, 8, C_QKV), lambda c: (seq(c), 0, 0))
    fixed = lambda c: (0, 0)
    return pl.pallas_call(
        functools.partial(_c_chunk_body, npch=npch, cps=cps, cpd=cpd),
        grid=(nt // CHUNK,),
        in_specs=[pl.BlockSpec((CHUNK, C_QKV), lambda c: (c, 0)),
                  pl.BlockSpec((CHUNK, z_w), lambda c: (c, C_QKV // z_w)),
                  pl.BlockSpec((CHUNK, 128), lambda c: (c, (C_QKV + z_w) // 128)),
                  pl.BlockSpec(cw.shape, fixed), pl.BlockSpec(alog.shape, fixed),
                  pl.BlockSpec(dtb.shape, fixed), pl.BlockSpec(onorm.shape, fixed), cv, st],
        out_specs=[pl.BlockSpec((CHUNK, z_w), lambda c: (c, 0)), cv, st],
        out_shape=[jax.ShapeDtypeStruct((nt, z_w), F32),
                   jax.ShapeDtypeStruct((nseq, 8, C_QKV), F32),
                   jax.ShapeDtypeStruct((nseq, C_HEADS, C_DK, C_DV), F32)],
        scratch_shapes=[pltpu.VMEM((C_HEADS, C_DK, C_DV), F32), pltpu.VMEM((8, C_QKV), F32)],
        compiler_params=pltpu.CompilerParams(
            dimension_semantics=("arbitrary",), vmem_limit_bytes=VMEM_LIMIT),
        name="gdn_chunk",
    )(proj, proj, proj, cw, alog, dtb, onorm, cv_in, s_in)


def _pad_cols(w, n):
    return jnp.pad(w, ((0, 0), (0, n - w.shape[1])))


def kernel(x_prompt, x_sample, state_a_shift, state_a_wkv, state_b_kv, state_c_conv, state_c_kv, norm_mix, norm_ffn, norm_final, ffn_w_in, ffn_w_out, a_mu, a_w0, a_w1, a_w2, a_a0, a_a1, a_a2, a_g1, a_g2, a_k_k, a_k_a, a_r_k, a_w_rkv, a_w_o, a_lnx_w, a_lnx_b, b_w_in, b_w_a1, b_w_a2, b_b_a, b_onorm, b_w_o, c_w_in, c_conv_w, c_a_log, c_dt_bias, c_onorm, c_w_o):
    bp, seq, d = x_prompt.shape
    bd, dseq, _ = x_sample.shape
    depth = norm_mix.shape[0]
    cps, cpd = seq // CHUNK, dseq // CHUNK
    npch = bp * cps
    geo = (npch, cps, cpd)
    nseq = bp + bd
    x = jnp.concatenate([x_prompt.reshape(bp * seq, d), x_sample.reshape(bd * dseq, d)], axis=0)
    row = lambda a: a.reshape(1, -1)
    bf = lambda a: a.astype(BF16)
    zeros_like_rows = lambda n, a: jnp.zeros((n,) + a.shape[1:], a.dtype)

    last_prompt = [(s + 1) * cps - 1 for s in range(bp)]
    last_sample = [npch + (s + 1) * cpd - 1 for s in range(bd)]

    out_a_shift, out_a_wkv, out_b_kv, out_c_conv, out_c_kv = [], [], [], [], []
    y_final = None
    for i in range(depth):
        j = i // 3
        gain = row(norm_mix[i])
        if i % 3 == 0:
            shift_rows = jnp.zeros((x.shape[0] // CHUNK, d), F32)
            shift_rows = shift_rows.at[jnp.asarray([npch + s * cpd for s in range(bd)])].set(state_a_shift[j])
            r, k, v, lw, al, gg, hl = _a_proj(
                x, shift_rows, gain, a_mu[j], row(a_w0[j]), row(a_a0[j]),
                bf(a_w_rkv[j, 0]), bf(a_w_rkv[j, 1]), bf(a_w_rkv[j, 2]),
                bf(a_w1[j]), bf(a_a1[j]), bf(a_g1[j]), bf(a_w2[j]), bf(a_a2[j]), bf(a_g2[j]), *geo)
            y, h_out = _a_chunk(r, k, v, lw, al, gg, row(a_k_k[j]), row(a_k_a[j]), row(a_r_k[j]),
                                row(a_lnx_w[j]), row(a_lnx_b[j]), _a_state_in(state_a_wkv[j], bp), *geo, bp)
            out_a_shift.append(hl[jnp.asarray(last_prompt + last_sample)])
            out_a_wkv.append(_a_state_out(h_out))
            w_o = a_w_o[j]
        elif i % 3 == 1:
            w = jnp.concatenate([b_w_in[j], _pad_cols(b_w_a1[j], B_LORA_PAD)], axis=1)
            proj = _norm_proj(x, gain, bf(w), 640)
            wa2 = jnp.pad(b_w_a2[j], ((0, B_LORA_PAD - b_w_a2.shape[1]), (0, 0)))
            s_in = jnp.concatenate([zeros_like_rows(bp, state_b_kv[j]), state_b_kv[j]], axis=0)
            y, s_out = _b_chunk(proj, bf(wa2), row(b_b_a[j]), row(b_onorm[j]), s_in, *geo, bp)
            out_b_kv.append(s_out)
            w_o = b_w_o[j]
        else:
            proj = _norm_proj(x, gain, bf(_pad_cols(c_w_in[j], C_PROJ_PAD)), 1408)
            lanes = lambda a: jnp.pad(a.reshape(1, -1), ((0, 0), (C_HEADS, 128 - 2 * C_HEADS)))
            cv_in = jnp.pad(state_c_conv[j], ((bp, 0), (8 - (C_CONV - 1), 0), (0, 0)))
            s_in = jnp.concatenate([zeros_like_rows(bp, state_c_kv[j]), state_c_kv[j]], axis=0)
            y, cv_out, s_out = _c_chunk(proj, c_conv_w[j], lanes(c_a_log[j]), lanes(c_dt_bias[j]),
                                        row(c_onorm[j]), cv_in, s_in, *geo, bp)
            out_c_conv.append(cv_out[:, 8 - (C_CONV - 1):])
            out_c_kv.append(s_out)
            w_o = c_w_o[j]
        final = i == depth - 1
        x, y_final = _out_ffn(x, y, bf(w_o), row(norm_ffn[i]), bf(ffn_w_in[i]), bf(ffn_w_out[i]),
                              row(norm_final), final)

    split = lambda parts: (jnp.stack([t[:bp] for t in parts]), jnp.stack([t[bp:] for t in parts]))
    a_shift_p, a_shift_s = split(out_a_shift)
    a_wkv_p, a_wkv_s = split(out_a_wkv)
    b_kv_p, b_kv_s = split(out_b_kv)
    c_conv_p, c_conv_s = split(out_c_conv)
    c_kv_p, c_kv_s = split(out_c_kv)
    y_prompt = y_final[:bp * seq].reshape(bp, seq, d)
    y_sample = y_final[bp * seq:].reshape(bd, dseq, d)
    return (y_prompt, y_sample, a_shift_p, a_wkv_p, b_kv_p, c_conv_p, c_kv_p,
            a_shift_s, a_wkv_s, b_kv_s, c_conv_s, c_kv_s)
```

```python
import functools

import jax
import jax.numpy as jnp
from jax import lax
from jax.experimental import pallas as pl
from jax.experimental.pallas import tpu as pltpu

F32 = jnp.float32
BF16 = jnp.bfloat16

D_MODEL = 1024
CHUNK = 64
RMS_EPS = 1e-6
D_FF = 2816
FFN_COLS = 256

A_HEADS = 16
A_HEAD = 64
A_PAIRS = A_HEADS // 2
A_LNX_EPS = 64e-5

B_HEADS = 4
B_DK = 128
B_DV = 256
B_GATE_TAU = 16.0
B_LORA_PAD = 128

C_HEADS = 8
C_DK = 128
C_DV = 128
C_CONV = 4
C_QKV = C_HEADS * (2 * C_DK + C_DV)
C_PROJ_PAD = C_QKV + C_HEADS * C_DV + 128

TOKEN_TILE = 512
VMEM_LIMIT = 56 * 1024 * 1024


def _mm(a, b):
    return jnp.dot(a.astype(BF16), b.astype(BF16), preferred_element_type=F32)


def _mm_nt(a, b):
    return lax.dot_general(a.astype(BF16), b.astype(BF16), (((1,), (1,)), ((), ())),
                           preferred_element_type=F32)


def _mm_tn(a, b):
    return lax.dot_general(a.astype(BF16), b.astype(BF16), (((0,), (0,)), ((), ())),
                           preferred_element_type=F32)


def _split3(x):
    hi = x.astype(BF16)
    r1 = x - hi.astype(F32)
    mid = r1.astype(BF16)
    lo = (r1 - mid.astype(F32)).astype(BF16)
    return hi, mid, lo


def _exact_left(m_bf16, x):
    hi, mid, lo = _split3(x)
    dot = lambda p: jnp.dot(m_bf16, p, preferred_element_type=F32)
    return dot(hi) + dot(mid) + dot(lo)


def _exact_tn(x, m_bf16):
    hi, mid, lo = _split3(x)
    dot = lambda p: lax.dot_general(p, m_bf16, (((0,), (0,)), ((), ())), preferred_element_type=F32)
    return dot(hi) + dot(mid) + dot(lo)


def _rms(x, gain):
    return x * lax.rsqrt(jnp.mean(x * x, axis=-1, keepdims=True) + RMS_EPS) * gain


def _softplus(x):
    return jnp.maximum(x, 0.0) + jnp.log1p(jnp.exp(-jnp.abs(x)))


def _sigmoid(x):
    return 1.0 / (1.0 + jnp.exp(-x))


def _silu(x):
    return x * _sigmoid(x)


def _iota(shape, dim):
    return lax.broadcasted_iota(jnp.int32, shape, dim)


def _seq_flags(c, npch, cps, cpd):
    start = jnp.where(c < npch, c % cps == 0, (c - npch) % cpd == 0)
    end = jnp.where(c < npch, c % cps == cps - 1, (c - npch) % cpd == cpd - 1)
    return start, end


def _seq_index(c, npch, cps, cpd, bp):
    return jnp.where(c < npch, c // cps, bp + (c - npch) // cpd)


def _proj_body(x_ref, gain_ref, w_ref, o_ref, h_ref):
    @pl.when(pl.program_id(1) == 0)
    def _():
        h_ref[...] = _rms(x_ref[...], gain_ref[...]).astype(BF16)

    o_ref[...] = jnp.dot(h_ref[...], w_ref[...], preferred_element_type=F32)


def _norm_proj(x, gain, w, tn):
    nt, n = x.shape[0], w.shape[1]
    tm = TOKEN_TILE
    return pl.pallas_call(
        _proj_body,
        grid=(nt // tm, n // tn),
        in_specs=[pl.BlockSpec((tm, D_MODEL), lambda i, j: (i, 0)),
                  pl.BlockSpec((1, D_MODEL), lambda i, j: (0, 0)),
                  pl.BlockSpec((D_MODEL, tn), lambda i, j: (0, j))],
        out_specs=pl.BlockSpec((tm, tn), lambda i, j: (i, j)),
        out_shape=jax.ShapeDtypeStruct((nt, n), F32),
        scratch_shapes=[pltpu.VMEM((tm, D_MODEL), BF16)],
        compiler_params=pltpu.CompilerParams(
            dimension_semantics=("arbitrary", "arbitrary"), vmem_limit_bytes=VMEM_LIMIT),
        name="norm_proj",
    )(x, gain, w)


def _ffn_body(x_ref, y_ref, wo_ref, gain_ref, win_ref, wout_ref, fgain_ref, o_ref, of_ref, acc_ref, *, final):
    x1 = x_ref[...] + jnp.dot(y_ref[...].astype(BF16), wo_ref[...], preferred_element_type=F32)
    h = _rms(x1, gain_ref[...]).astype(BF16)
    acc_ref[...] = x1
    for c in range(D_FF // FFN_COLS):
        lo = c * FFN_COLS
        g = jnp.dot(h, win_ref[:, lo:lo + FFN_COLS], preferred_element_type=F32)
        u = jnp.dot(h, win_ref[:, D_FF + lo:D_FF + lo + FFN_COLS], preferred_element_type=F32)
        a = (_silu(g) * u).astype(BF16)
        acc_ref[...] += jnp.dot(a, wout_ref[lo:lo + FFN_COLS, :], preferred_element_type=F32)
    out = acc_ref[...]
    o_ref[...] = out
    if final:
        of_ref[...] = _rms(out, fgain_ref[...])
    else:
        of_ref[...] = jnp.zeros_like(of_ref)


def _out_ffn(x, y, wo, gain, win, wout, fgain, final):
    nt = x.shape[0]
    tm = TOKEN_TILE
    row = lambda i: (i, 0)
    fixed = lambda i: (0, 0)
    of_shape = (nt, D_MODEL) if final else (8, 128)
    of_spec = pl.BlockSpec((tm, D_MODEL), row) if final else pl.BlockSpec((8, 128), fixed)
    return pl.pallas_call(
        functools.partial(_ffn_body, final=final),
        grid=(nt // tm,),
        in_specs=[pl.BlockSpec((tm, D_MODEL), row),
                  pl.BlockSpec((tm, D_MODEL), row),
                  pl.BlockSpec((D_MODEL, D_MODEL), fixed),
                  pl.BlockSpec((1, D_MODEL), fixed),
                  pl.BlockSpec((D_MODEL, 2 * D_FF), fixed),
                  pl.BlockSpec((D_FF, D_MODEL), fixed),
                  pl.BlockSpec((1, D_MODEL), fixed)],
        out_specs=[pl.BlockSpec((tm, D_MODEL), row), of_spec],
        out_shape=[jax.ShapeDtypeStruct((nt, D_MODEL), F32), jax.ShapeDtypeStruct(of_shape, F32)],
        scratch_shapes=[pltpu.VMEM((tm, D_MODEL), F32)],
        compiler_params=pltpu.CompilerParams(
            dimension_semantics=("arbitrary",), vmem_limit_bytes=VMEM_LIMIT),
        name="out_ffn",
    )(x, y, wo, gain, win, wout, fgain)


def _a_proj_body(x_ref, xp_ref, sh_ref, gain_ref, mu_ref, w0_ref, a0_ref,
                 wr_ref, wk_ref, wv_ref, w1_ref, a1_ref, g1_ref, w2_ref, a2_ref, g2_ref,
                 r_ref, k_ref, v_ref, lw_ref, al_ref, gg_ref, hl_ref, *, tm, npch, cps, cpd):
    i = pl.program_id(0)
    gain = gain_ref[...]
    h = _rms(x_ref[...], gain)
    prev_tile_last = _rms(xp_ref[...], gain)[7:8, :]
    row = _iota((CHUNK, 1), 0)
    pieces = []
    for c in range(tm // CHUNK):
        hc = h[c * CHUNK:(c + 1) * CHUNK]
        natural = prev_tile_last if c == 0 else h[c * CHUNK - 1:c * CHUNK]
        start, _ = _seq_flags(i * (tm // CHUNK) + c, npch, cps, cpd)
        first = jnp.where(start, sh_ref[c:c + 1, :], natural)
        pieces.append(jnp.where(row == 0, first, pltpu.roll(hc, 1, 0)))
        hl_ref[c:c + 1, :] = hc[CHUNK - 1:CHUNK]
    xx = jnp.concatenate(pieces, axis=0) - h
    mix = lambda j: h + xx * mu_ref[j:j + 1, :]
    r_ref[...] = _mm(mix(0), wr_ref[...])
    log_rate = -_softplus(-(w0_ref[...] + _mm(jnp.tanh(_mm(mix(1), w1_ref[...])), w2_ref[...]))) - 0.5
    lw_ref[...] = -jnp.exp(log_rate)
    k_ref[...] = _mm(mix(2), wk_ref[...])
    v_ref[...] = _mm(mix(3), wv_ref[...])
    al_ref[...] = _sigmoid(a0_ref[...] + _mm(_mm(mix(4), a1_ref[...]), a2_ref[...]))
    gg_ref[...] = _mm(_sigmoid(_mm(mix(5), g1_ref[...])), g2_ref[...])


def _a_proj(x, shift_rows, gain, mu, w0, a0, wr, wk, wv, w1, a1, g1, w2, a2, g2, npch, cps, cpd):
    nt = x.shape[0]
    tm = TOKEN_TILE
    row = lambda i: (i, 0)
    fixed = lambda i: (0, 0)
    full = lambda a: pl.BlockSpec(a.shape, fixed)
    tok = pl.BlockSpec((tm, D_MODEL), row)
    tok_shape = jax.ShapeDtypeStruct((nt, D_MODEL), F32)
    return pl.pallas_call(
        functools.partial(_a_proj_body, tm=tm, npch=npch, cps=cps, cpd=cpd),
        grid=(nt // tm,),
        in_specs=[tok,
                  pl.BlockSpec((8, D_MODEL), lambda i: (jnp.maximum(i * (tm // 8) - 1, 0), 0)),
                  pl.BlockSpec((tm // CHUNK, D_MODEL), row),
                  full(gain), full(mu), full(w0), full(a0), full(wr), full(wk), full(wv),
                  full(w1), full(a1), full(g1), full(w2), full(a2), full(g2)],
        out_specs=[tok] * 6 + [pl.BlockSpec((tm // CHUNK, D_MODEL), row)],
        out_shape=[tok_shape] * 6 + [jax.ShapeDtypeStruct((nt // CHUNK, D_MODEL), F32)],
        compiler_params=pltpu.CompilerParams(
            dimension_semantics=("arbitrary",), vmem_limit_bytes=VMEM_LIMIT),
        name="rwkv_proj",
    )(x, x, shift_rows, gain, mu, w0, a0, wr, wk, wv, w1, a1, g1, w2, a2, g2)


def _a_chunk_body(r_ref, k_ref, v_ref, lw_ref, al_ref, gg_ref, kk_ref, ka_ref, rk_ref, lnw_ref, lnb_ref,
                  sin_ref, y_ref, sout_ref, ss_ref, *, npch, cps, cpd):
    c = pl.program_id(0)
    start, end = _seq_flags(c, npch, cps, cpd)

    @pl.when(start)
    def _():
        zero = jnp.zeros((A_HEAD, A_HEAD), F32)
        for p in range(A_PAIRS):
            ss_ref[p] = jnp.concatenate(
                [jnp.concatenate([sin_ref[0, 2 * p], zero], axis=1),
                 jnp.concatenate([zero, sin_ref[0, 2 * p + 1]], axis=1)], axis=0)

    lane = _iota((1, 128), 1)
    head0 = lane < A_HEAD
    ri = _iota((128, 128), 0) % CHUNK
    ci = _iota((128, 128), 1) % CHUNK
    strict = ri > ci
    incl = ri >= ci
    tri = (_iota((CHUNK, CHUNK), 0) >= _iota((CHUNK, CHUNK), 1)).astype(BF16)

    def headsum(x):
        s0 = jnp.sum(jnp.where(head0, x, 0.0), axis=-1, keepdims=True)
        s1 = jnp.sum(jnp.where(head0, 0.0, x), axis=-1, keepdims=True)
        return jnp.where(head0, s0, s1)

    def expand(x):
        return jnp.concatenate([jnp.where(head0, x, 0.0), jnp.where(head0, 0.0, x)], axis=0)

    pairs = range(A_PAIRS)
    sls = [slice(p * 128, (p + 1) * 128) for p in pairs]
    gcum_all = _exact_left(tri, lw_ref[...])
    xs, ys, ves, ends, glasts, kmods = [], [], [], [], [], []
    for sl in sls:
        k = k_ref[:, sl]
        lw = lw_ref[:, sl]
        al = al_ref[:, sl]
        kk = k * kk_ref[:, sl]
        kk = kk * lax.rsqrt(jnp.maximum(headsum(kk * kk), 1e-24))
        kmod = k * (1.0 + (al - 1.0) * ka_ref[:, sl])
        gcum = gcum_all[:, sl]
        glast = gcum[CHUNK - 1:CHUNK, :]
        dec_out = jnp.exp(-gcum)
        dec_end = jnp.exp(glast - gcum)
        ae = expand(-kk * jnp.exp(gcum - lw))
        re = expand(r_ref[:, sl] * jnp.exp(gcum))
        be = expand(kk * al * dec_out)
        ke = expand(kmod * dec_out)
        xs.append(jnp.concatenate([ae, re], axis=0))
        ys.append(jnp.concatenate([be, ke], axis=0))
        ves.append(expand(v_ref[:, sl]))
        ends.append(jnp.concatenate([expand(kk * al * dec_end), expand(kmod * dec_end)], axis=0))
        glasts.append(glast)
        kmods.append(kmod)

    sc = [_mm_nt(x, y) for x, y in zip(xs, ys)]
    n_ab = [jnp.where(strict, s[:128, :128], 0.0) for s in sc]
    l_ak = [jnp.where(strict, s[:128, 128:], 0.0) for s in sc]
    m_rb = [jnp.where(incl, s[128:, :128], 0.0) for s in sc]
    m_rk = [jnp.where(incl, s[128:, 128:], 0.0) for s in sc]

    eye = (_iota((128, 128), 0) == _iota((128, 128), 1)).astype(F32)
    tinv = [eye + n for n in n_ab]
    pw = n_ab
    for _ in range(5):
        pw = [_mm(m, m) for m in pw]
        tinv = [t + _mm(t, m) for t, m in zip(tinv, pw)]

    lv = [_mm(l, ve) for l, ve in zip(l_ak, ves)]
    pq = [_mm(t, jnp.concatenate([x[:128], l], axis=1)) for t, x, l in zip(tinv, xs, lv)]
    mt = [_mm(m, z) for m, z in zip(m_rb, pq)]
    mv = [_mm(m, ve) for m, ve in zip(m_rk, ves)]

    s0 = [ss_ref[p] for p in pairs]
    lhs = [jnp.concatenate([x[128:] + t[:, :128], z[:, :128]], axis=0) for x, t, z in zip(xs, mt, pq)]
    ou = [_mm_nt(a, s) for a, s in zip(lhs, s0)]
    oe = [o[:128] + t[:, 128:] + m for o, t, m in zip(ou, mt, mv)]
    uv = [jnp.concatenate([o[128:] + z[:, 128:], ve], axis=0) for o, z, ve in zip(ou, pq, ves)]
    upd = [_mm_tn(a, e) for a, e in zip(uv, ends)]
    for p in pairs:
        ss_ref[p] = s0[p] * jnp.exp(glasts[p]) + upd[p]

    for p, sl in enumerate(sls):
        o = oe[p][:CHUNK] + oe[p][CHUNK:]
        mean = headsum(o) * (1.0 / A_HEAD)
        cen = o - mean
        var = headsum(cen * cen) * (1.0 / A_HEAD)
        on = cen * lax.rsqrt(var + A_LNX_EPS) * lnw_ref[:, sl] + lnb_ref[:, sl]
        bonus = headsum(r_ref[:, sl] * kmods[p] * rk_ref[:, sl]) * v_ref[:, sl]
        y_ref[:, sl] = (on + bonus) * gg_ref[:, sl]

    @pl.when(end)
    def _():
        for p in range(A_PAIRS):
            s = ss_ref[p]
            sout_ref[0, 2 * p] = s[:A_HEAD, :A_HEAD]
            sout_ref[0, 2 * p + 1] = s[A_HEAD:, A_HEAD:]


def _a_chunk(r, k, v, lw, al, gg, k_k, k_a, r_k, lnx_w, lnx_b, s_in, npch, cps, cpd, bp):
    nt = r.shape[0]
    nseq = s_in.shape[0]
    tok = pl.BlockSpec((CHUNK, D_MODEL), lambda c: (c, 0))
    vec = pl.BlockSpec((1, D_MODEL), lambda c: (0, 0))
    st = pl.BlockSpec((1, A_HEADS, A_HEAD, A_HEAD), lambda c: (_seq_index(c, npch, cps, cpd, bp), 0, 0, 0))
    return pl.pallas_call(
        functools.partial(_a_chunk_body, npch=npch, cps=cps, cpd=cpd),
        grid=(nt // CHUNK,),
        in_specs=[tok] * 6 + [vec] * 5 + [st],
        out_specs=[tok, st],
        out_shape=[jax.ShapeDtypeStruct((nt, D_MODEL), F32),
                   jax.ShapeDtypeStruct((nseq, A_HEADS, A_HEAD, A_HEAD), F32)],
        scratch_shapes=[pltpu.VMEM((A_PAIRS, 128, 128), F32)],
        compiler_params=pltpu.CompilerParams(
            dimension_semantics=("arbitrary",), vmem_limit_bytes=VMEM_LIMIT),
        name="rwkv_chunk",
    )(r, k, v, lw, al, gg, k_k, k_a, r_k, lnx_w, lnx_b, s_in)


def _head_rms_gate(o, gain, gate):
    return o * lax.rsqrt(jnp.mean(o * o, axis=-1, keepdims=True) + RMS_EPS) * gain * _silu(gate)


def _b_chunk_body(q_ref, k_ref, v_ref, gate_ref, lo_ref, wa2_ref, ba_ref, onorm_ref, sin_ref,
                  y_ref, sout_ref, ss_ref, *, npch, cps, cpd):
    c = pl.program_id(0)
    start, end = _seq_flags(c, npch, cps, cpd)

    @pl.when(start)
    def _():
        ss_ref[...] = sin_ref[0]

    incl = _iota((CHUNK, CHUNK), 0) >= _iota((CHUNK, CHUNK), 1)
    tri = incl.astype(BF16)
    gk_all = -_softplus(-(_mm(lo_ref[...], wa2_ref[...]) + ba_ref[...])) * (1.0 / B_GATE_TAU)
    b_all = _exact_left(tri, gk_all)
    heads = range(B_HEADS)
    kss = [slice(h * B_DK, (h + 1) * B_DK) for h in heads]
    vss = [slice(h * B_DV, (h + 1) * B_DV) for h in heads]
    qt, kt, khat, blast = [], [], [], []
    for ks in kss:
        b = b_all[:, ks]
        k = k_ref[:, ks]
        blast.append(b[CHUNK - 1:CHUNK, :])
        qt.append(q_ref[:, ks] * (B_DK ** -0.5) * jnp.exp(b))
        kt.append(k * jnp.exp(-b))
        khat.append(k * jnp.exp(blast[-1] - b))
    a = [jnp.where(incl, _mm_nt(x, y), 0.0) for x, y in zip(qt, kt)]
    s0 = [ss_ref[h] for h in heads]
    o = [_mm(a[h], v_ref[:, vss[h]]) + _mm_nt(qt[h], s0[h]) for h in heads]
    upd = [_mm_tn(v_ref[:, vss[h]], khat[h]) for h in heads]
    for h in heads:
        ss_ref[h] = s0[h] * jnp.exp(blast[h]) + upd[h]
        y_ref[:, vss[h]] = _head_rms_gate(o[h], onorm_ref[...], gate_ref[:, vss[h]])

    @pl.when(end)
    def _():
        sout_ref[0] = ss_ref[...]


def _b_chunk(proj, wa2, ba, onorm, s_in, npch, cps, cpd, bp):
    nt = proj.shape[0]
    nseq = s_in.shape[0]
    qk_w = B_HEADS * B_DK
    v_w = B_HEADS * B_DV
    st = pl.BlockSpec((1, B_HEADS, B_DV, B_DK), lambda c: (_seq_index(c, npch, cps, cpd, bp), 0, 0, 0))
    fixed = lambda c: (0, 0)
    return pl.pallas_call(
        functools.partial(_b_chunk_body, npch=npch, cps=cps, cpd=cpd),
        grid=(nt // CHUNK,),
        in_specs=[pl.BlockSpec((CHUNK, qk_w), lambda c: (c, 0)),
                  pl.BlockSpec((CHUNK, qk_w), lambda c: (c, 1)),
                  pl.BlockSpec((CHUNK, v_w), lambda c: (c, 1)),
                  pl.BlockSpec((CHUNK, v_w), lambda c: (c, 2)),
                  pl.BlockSpec((CHUNK, B_LORA_PAD), lambda c: (c, (2 * qk_w + 2 * v_w) // B_LORA_PAD)),
                  pl.BlockSpec(wa2.shape, fixed), pl.BlockSpec(ba.shape, fixed),
                  pl.BlockSpec(onorm.shape, fixed), st],
        out_specs=[pl.BlockSpec((CHUNK, v_w), lambda c: (c, 0)), st],
        out_shape=[jax.ShapeDtypeStruct((nt, v_w), F32),
                   jax.ShapeDtypeStruct((nseq, B_HEADS, B_DV, B_DK), F32)],
        scratch_shapes=[pltpu.VMEM((B_HEADS, B_DV, B_DK), F32)],
        compiler_params=pltpu.CompilerParams(
            dimension_semantics=("arbitrary",), vmem_limit_bytes=VMEM_LIMIT),
        name="gla_chunk",
    )(proj, proj, proj, proj, proj, wa2, ba, onorm, s_in)


def _c_chunk_body(qkv_ref, z_ref, ba_ref, cw_ref, alog_ref, dtb_ref, onorm_ref, cvin_ref, sin_ref,
                  y_ref, cvout_ref, sout_ref, ss_ref, pv_ref, *, npch, cps, cpd):
    c = pl.program_id(0)
    start, end = _seq_flags(c, npch, cps, cpd)

    @pl.when(start)
    def _():
        ss_ref[...] = sin_ref[0]
        pv_ref[...] = cvin_ref[0]

    x = qkv_ref[...]
    pv = pv_ref[...]
    row8 = _iota((8, 1), 0)
    conv = x * cw_ref[C_CONV - 1:C_CONV, :]
    for j in range(1, C_CONV):
        sh = pltpu.roll(x, j, 0)
        head = jnp.where(row8 < j, pltpu.roll(pv, j, 0), sh[:8])
        sh = jnp.concatenate([head, sh[8:]], axis=0)
        conv = conv + sh * cw_ref[C_CONV - 1 - j:C_CONV - j, :]
    pv_ref[...] = x[CHUNK - 8:]
    act = _silu(conv)

    ba = ba_ref[...]
    beta_all = _sigmoid(ba)
    g_all = -jnp.exp(alog_ref[...]) * _softplus(ba + dtb_ref[...])
    tri = (_iota((CHUNK, CHUNK), 0) >= _iota((CHUNK, CHUNK), 1)).astype(BF16)
    triu2 = (_iota((CHUNK, 128), 0) <= _iota((CHUNK, 128), 1) % CHUNK).astype(BF16)
    gc_col = _exact_left(tri, g_all)
    gc_row = _exact_tn(g_all, triu2)

    ri = _iota((128, 128), 0)
    ci = _iota((128, 128), 1)
    same = (ri // CHUNK) == (ci // CHUNK)
    strict = same & (ri % CHUNK > ci % CHUNK)
    incl = same & (ri % CHUNK >= ci % CHUNK)
    lane_lo = _iota((1, 128), 1) < CHUNK
    zeros = jnp.zeros((CHUNK, C_DK), F32)
    l2 = lambda t: t * lax.rsqrt(jnp.sum(t * t, axis=-1, keepdims=True) + 1e-6)

    heads = range(C_HEADS)
    q = [l2(act[:, h * C_DK:(h + 1) * C_DK]) * (C_DK ** -0.5) for h in heads]
    k = [l2(act[:, C_HEADS * C_DK + h * C_DK:C_HEADS * C_DK + (h + 1) * C_DK]) for h in heads]
    v = [act[:, 2 * C_HEADS * C_DK + h * C_DV:2 * C_HEADS * C_DK + (h + 1) * C_DV] for h in heads]
    beta = [beta_all[:, h:h + 1] for h in heads]
    kb = [k[h] * beta[h] for h in heads]
    col = [gc_col[:, C_HEADS + h:C_HEADS + h + 1] for h in heads]
    egc = [jnp.exp(cc) for cc in col]
    glast = [cc[CHUNK - 1:CHUNK, :] for cc in col]
    pairs = [(2 * p, 2 * p + 1) for p in range(C_HEADS // 2)]
    blockdiag = lambda a, b: jnp.concatenate(
        [jnp.concatenate([a, zeros], axis=1), jnp.concatenate([zeros, b], axis=1)], axis=0)

    decay, sc, rhs = [], [], []
    for h0, h1 in pairs:
        col_pair = jnp.concatenate([col[h0], col[h1]], axis=0)
        row_pair = jnp.where(lane_lo, gc_row[C_HEADS + h0:C_HEADS + h0 + 1, :],
                             gc_row[C_HEADS + h1:C_HEADS + h1 + 1, :])
        decay.append(jnp.where(incl, jnp.exp(jnp.where(incl, col_pair - row_pair, 0.0)), 0.0))
        lhs = jnp.concatenate([blockdiag(kb[h0], kb[h1]), blockdiag(q[h0], q[h1])], axis=0)
        sc.append(_mm_nt(lhs, blockdiag(k[h0], k[h1])))
        rhs.append(jnp.concatenate(
            [jnp.concatenate([v[h] * beta[h], kb[h] * egc[h]], axis=1) for h in (h0, h1)], axis=0))
    n_l = [-jnp.where(strict, s[:128], 0.0) * d for s, d in zip(sc, decay)]
    attn = [s[128:] * d for s, d in zip(sc, decay)]

    eye = (ri == ci).astype(F32)
    tinv = [eye + n for n in n_l]
    pw = n_l
    for _ in range(5):
        pw = [_mm(m, m) for m in pw]
        tinv = [t + _mm(t, m) for t, m in zip(tinv, pw)]
    sol = [_mm(t, r) for t, r in zip(tinv, rhs)]

    s0 = [ss_ref[h] for h in heads]
    qw = [_mm(jnp.concatenate([q[h] * egc[h],
                               sol[h // 2][(h % 2) * CHUNK:(h % 2 + 1) * CHUNK, C_DV:]], axis=0), s0[h])
          for h in heads]
    v_new = [sol[p][:, :C_DV] - jnp.concatenate([qw[h0][CHUNK:], qw[h1][CHUNK:]], axis=0)
             for p, (h0, h1) in enumerate(pairs)]
    o_intra = [_mm(a, vn) for a, vn in zip(attn, v_new)]
    upd = [_mm_tn(k[h] * jnp.exp(glast[h] - col[h]), v_new[h // 2][(h % 2) * CHUNK:(h % 2 + 1) * CHUNK])
           for h in heads]
    for h in heads:
        o = qw[h][:CHUNK] + o_intra[h // 2][(h % 2) * CHUNK:(h % 2 + 1) * CHUNK]
        ss_ref[h] = s0[h] * jnp.exp(glast[h]) + upd[h]
        vs = slice(h * C_DV, (h + 1) * C_DV)
        y_ref[:, vs] = _head_rms_gate(o, onorm_ref[...], z_ref[:, vs])

    @pl.when(end)
    def _():
        sout_ref[0] = ss_ref[...]
        cvout_ref[0] = pv_ref[...]


def _c_chunk(proj, cw, alog, dtb, onorm, cv_in, s_in, npch, cps, cpd, bp):
    nt = proj.shape[0]
    nseq = s_in.shape[0]
    z_w = C_HEADS * C_DV
    seq = lambda c: _seq_index(c, npch, cps, cpd, bp)
    st = pl.BlockSpec((1, C_HEADS, C_DK, C_DV), lambda c: (seq(c), 0, 0, 0))
    cv = pl.BlockSpec((1, 8, C_QKV), lambda c: (seq(c), 0, 0))
    fixed = lambda c: (0, 0)
    return pl.pallas_call(
        functools.partial(_c_chunk_body, npch=npch, cps=cps, cpd=cpd),
        grid=(nt // CHUNK,),
        in_specs=[pl.BlockSpec((CHUNK, C_QKV), lambda c: (c, 0)),
                  pl.BlockSpec((CHUNK, z_w), lambda c: (c, C_QKV // z_w)),
                  pl.BlockSpec((CHUNK, 128), lambda c: (c, (C_QKV + z_w) // 128)),
                  pl.BlockSpec(cw.shape, fixed), pl.BlockSpec(alog.shape, fixed),
                  pl.BlockSpec(dtb.shape, fixed), pl.BlockSpec(onorm.shape, fixed), cv, st],
        out_specs=[pl.BlockSpec((CHUNK, z_w), lambda c: (c, 0)), cv, st],
        out_shape=[jax.ShapeDtypeStruct((nt, z_w), F32),
                   jax.ShapeDtypeStruct((nseq, 8, C_QKV), F32),
                   jax.ShapeDtypeStruct((nseq, C_HEADS, C_DK, C_DV), F32)],
        scratch_shapes=[pltpu.VMEM((C_HEADS, C_DK, C_DV), F32), pltpu.VMEM((8, C_QKV), F32)],
        compiler_params=pltpu.CompilerParams(
            dimension_semantics=("arbitrary",), vmem_limit_bytes=VMEM_LIMIT),
        name="gdn_chunk",
    )(proj, proj, proj, cw, alog, dtb, onorm, cv_in, s_in)


def _pad_cols(w, n):
    return jnp.pad(w, ((0, 0), (0, n - w.shape[1])))


def kernel(x_prompt, x_sample, state_a_shift, state_a_wkv, state_b_kv, state_c_conv, state_c_kv, norm_mix, norm_ffn, norm_final, ffn_w_in, ffn_w_out, a_mu, a_w0, a_w1, a_w2, a_a0, a_a1, a_a2, a_g1, a_g2, a_k_k, a_k_a, a_r_k, a_w_rkv, a_w_o, a_lnx_w, a_lnx_b, b_w_in, b_w_a1, b_w_a2, b_b_a, b_onorm, b_w_o, c_w_in, c_conv_w, c_a_log, c_dt_bias, c_onorm, c_w_o):
    bp, seq, d = x_prompt.shape
    bd, dseq, _ = x_sample.shape
    depth = norm_mix.shape[0]
    cps, cpd = seq // CHUNK, dseq // CHUNK
    npch = bp * cps
    geo = (npch, cps, cpd)
    nseq = bp + bd
    x = jnp.concatenate([x_prompt.reshape(bp * seq, d), x_sample.reshape(bd * dseq, d)], axis=0)
    row = lambda a: a.reshape(1, -1)
    bf = lambda a: a.astype(BF16)
    zeros_like_rows = lambda n, a: jnp.zeros((n,) + a.shape[1:], a.dtype)

    last_prompt = [(s + 1) * cps - 1 for s in range(bp)]
    last_sample = [npch + (s + 1) * cpd - 1 for s in range(bd)]

    out_a_shift, out_a_wkv, out_b_kv, out_c_conv, out_c_kv = [], [], [], [], []
    y_final = None
    for i in range(depth):
        j = i // 3
        gain = row(norm_mix[i])
        if i % 3 == 0:
            shift_rows = jnp.zeros((x.shape[0] // CHUNK, d), F32)
            shift_rows = shift_rows.at[jnp.asarray([npch + s * cpd for s in range(bd)])].set(state_a_shift[j])
            r, k, v, lw, al, gg, hl = _a_proj(
                x, shift_rows, gain, a_mu[j], row(a_w0[j]), row(a_a0[j]),
                bf(a_w_rkv[j, 0]), bf(a_w_rkv[j, 1]), bf(a_w_rkv[j, 2]),
                bf(a_w1[j]), bf(a_a1[j]), bf(a_g1[j]), bf(a_w2[j]), bf(a_a2[j]), bf(a_g2[j]), *geo)
            s_in = jnp.concatenate([zeros_like_rows(bp, state_a_wkv[j]), state_a_wkv[j]], axis=0)
            y, s_out = _a_chunk(r, k, v, lw, al, gg, row(a_k_k[j]), row(a_k_a[j]), row(a_r_k[j]),
                                row(a_lnx_w[j]), row(a_lnx_b[j]), s_in, *geo, bp)
            out_a_shift.append(hl[jnp.asarray(last_prompt + last_sample)])
            out_a_wkv.append(s_out)
            w_o = a_w_o[j]
        elif i % 3 == 1:
            w = jnp.concatenate([b_w_in[j], _pad_cols(b_w_a1[j], B_LORA_PAD)], axis=1)
            proj = _norm_proj(x, gain, bf(w), 640)
            wa2 = jnp.pad(b_w_a2[j], ((0, B_LORA_PAD - b_w_a2.shape[1]), (0, 0)))
            st = jnp.swapaxes(state_b_kv[j], -1, -2)
            s_in = jnp.concatenate([zeros_like_rows(bp, st), st], axis=0)
            y, s_out = _b_chunk(proj, bf(wa2), row(b_b_a[j]), row(b_onorm[j]), s_in, *geo, bp)
            out_b_kv.append(jnp.swapaxes(s_out, -1, -2))
            w_o = b_w_o[j]
        else:
            proj = _norm_proj(x, gain, bf(_pad_cols(c_w_in[j], C_PROJ_PAD)), 1408)
            lanes = lambda a: jnp.pad(a.reshape(1, -1), ((0, 0), (C_HEADS, 128 - 2 * C_HEADS)))
            cv_in = jnp.pad(state_c_conv[j], ((bp, 0), (8 - (C_CONV - 1), 0), (0, 0)))
            s_in = jnp.concatenate([zeros_like_rows(bp, state_c_kv[j]), state_c_kv[j]], axis=0)
            y, cv_out, s_out = _c_chunk(proj, c_conv_w[j], lanes(c_a_log[j]), lanes(c_dt_bias[j]),
                                        row(c_onorm[j]), cv_in, s_in, *geo, bp)
            out_c_conv.append(cv_out[:, 8 - (C_CONV - 1):])
            out_c_kv.append(s_out)
            w_o = c_w_o[j]
        final = i == depth - 1
        x, y_final = _out_ffn(x, y, bf(w_o), row(norm_ffn[i]), bf(ffn_w_in[i]), bf(ffn_w_out[i]),
                              row(norm_final), final)

    split = lambda parts: (jnp.stack([t[:bp] for t in parts]), jnp.stack([t[bp:] for t in parts]))
    a_shift_p, a_shift_s = split(out_a_shift)
    a_wkv_p, a_wkv_s = split(out_a_wkv)
    b_kv_p, b_kv_s = split(out_b_kv)
    c_conv_p, c_conv_s = split(out_c_conv)
    c_kv_p, c_kv_s = split(out_c_kv)
    y_prompt = y_final[:bp * seq].reshape(bp, seq, d)
    y_sample = y_final[bp * seq:].reshape(bd, dseq, d)
    return (y_prompt, y_sample, a_shift_p, a_wkv_p, b_kv_p, c_conv_p, c_kv_p,
            a_shift_s, a_wkv_s, b_kv_s, c_conv_s, c_kv_s)
```

```python
import functools

import jax
import jax.numpy as jnp
from jax import lax
from jax.experimental import pallas as pl
from jax.experimental.pallas import tpu as pltpu

F32 = jnp.float32
BF16 = jnp.bfloat16

D_MODEL = 1024
CHUNK = 64
RMS_EPS = 1e-6
D_FF = 2816
FFN_COLS = 256

A_HEADS = 16
A_HEAD = 64
A_PAIRS = A_HEADS // 2
A_LNX_EPS = 64e-5

B_HEADS = 4
B_DK = 128
B_DV = 256
B_GATE_TAU = 16.0
B_LORA_PAD = 128

C_HEADS = 8
C_DK = 128
C_DV = 128
C_CONV = 4
C_QKV = C_HEADS * (2 * C_DK + C_DV)
C_PROJ_PAD = C_QKV + C_HEADS * C_DV + 128

TOKEN_TILE = 512
CHUNKS_PER_STEP = 2
VMEM_LIMIT = 56 * 1024 * 1024


def _mm(a, b):
    return jnp.dot(a.astype(BF16), b.astype(BF16), preferred_element_type=F32)


def _mm_nt(a, b):
    return lax.dot_general(a.astype(BF16), b.astype(BF16), (((1,), (1,)), ((), ())),
                           preferred_element_type=F32)


def _mm_tn(a, b):
    return lax.dot_general(a.astype(BF16), b.astype(BF16), (((0,), (0,)), ((), ())),
                           preferred_element_type=F32)


def _split3(x):
    hi = x.astype(BF16)
    r1 = x - hi.astype(F32)
    mid = r1.astype(BF16)
    lo = (r1 - mid.astype(F32)).astype(BF16)
    return hi, mid, lo


def _exact_left(m_bf16, x):
    hi, mid, lo = _split3(x)
    dot = lambda p: jnp.dot(m_bf16, p, preferred_element_type=F32)
    return dot(hi) + dot(mid) + dot(lo)


def _exact_tn(x, m_bf16):
    hi, mid, lo = _split3(x)
    dot = lambda p: lax.dot_general(p, m_bf16, (((0,), (0,)), ((), ())), preferred_element_type=F32)
    return dot(hi) + dot(mid) + dot(lo)


def _rms(x, gain):
    return x * lax.rsqrt(jnp.mean(x * x, axis=-1, keepdims=True) + RMS_EPS) * gain


def _softplus(x):
    return jnp.maximum(x, 0.0) + jnp.log1p(jnp.exp(-jnp.abs(x)))


def _sigmoid(x):
    return 1.0 / (1.0 + jnp.exp(-x))


def _silu(x):
    return x * _sigmoid(x)


def _iota(shape, dim):
    return lax.broadcasted_iota(jnp.int32, shape, dim)


def _nilpotent_inverse(ns):
    eye = (_iota((128, 128), 0) == _iota((128, 128), 1)).astype(F32)
    inv = [eye + n for n in ns]
    pw = [n.astype(BF16) for n in ns]
    for _ in range(5):
        pw = [_mm(m, m).astype(BF16) for m in pw]
        inv = [t + _mm(t, m) for t, m in zip(inv, pw)]
    return [t.astype(BF16) for t in inv]


def _chunk_rows(j):
    return slice(j * CHUNK, (j + 1) * CHUNK)


def _is_first_step():
    return pl.program_id(1) == 0


def _is_last_step():
    return pl.program_id(1) == pl.num_programs(1) - 1


def _proj_body(x_ref, gain_ref, w_ref, o_ref, h_ref):
    @pl.when(pl.program_id(1) == 0)
    def _():
        h_ref[...] = _rms(x_ref[...], gain_ref[...]).astype(BF16)

    o_ref[...] = jnp.dot(h_ref[...], w_ref[...], preferred_element_type=F32)


def _norm_proj(x, gain, w, tn):
    nt, n = x.shape[0], w.shape[1]
    tm = TOKEN_TILE
    return pl.pallas_call(
        _proj_body,
        grid=(nt // tm, n // tn),
        in_specs=[pl.BlockSpec((tm, D_MODEL), lambda i, j: (i, 0)),
                  pl.BlockSpec((1, D_MODEL), lambda i, j: (0, 0)),
                  pl.BlockSpec((D_MODEL, tn), lambda i, j: (0, j))],
        out_specs=pl.BlockSpec((tm, tn), lambda i, j: (i, j)),
        out_shape=jax.ShapeDtypeStruct((nt, n), F32),
        scratch_shapes=[pltpu.VMEM((tm, D_MODEL), BF16)],
        compiler_params=pltpu.CompilerParams(
            dimension_semantics=("arbitrary", "arbitrary"), vmem_limit_bytes=VMEM_LIMIT),
        name="norm_proj",
    )(x, gain, w)


def _ffn_body(x_ref, yp_ref, ys_ref, wo_ref, gain_ref, win_ref, wout_ref, fgain_ref, *rest, n_prompt_tiles, final):
    if final:
        fp_ref, fs_ref, acc_ref = rest
    else:
        o_ref, acc_ref = rest
    in_prompt = pl.program_id(0) < n_prompt_tiles
    y = jnp.where(in_prompt, yp_ref[...], ys_ref[...])
    x1 = x_ref[...] + jnp.dot(y.astype(BF16), wo_ref[...], preferred_element_type=F32)
    h = _rms(x1, gain_ref[...]).astype(BF16)
    acc_ref[...] = x1
    for c in range(D_FF // FFN_COLS):
        lo = c * FFN_COLS
        g = jnp.dot(h, win_ref[:, lo:lo + FFN_COLS], preferred_element_type=F32)
        u = jnp.dot(h, win_ref[:, D_FF + lo:D_FF + lo + FFN_COLS], preferred_element_type=F32)
        a = (_silu(g) * u).astype(BF16)
        acc_ref[...] += jnp.dot(a, wout_ref[lo:lo + FFN_COLS, :], preferred_element_type=F32)
    if final:
        @pl.when(in_prompt)
        def _():
            fp_ref[...] = _rms(acc_ref[...], fgain_ref[...])

        @pl.when(jnp.logical_not(in_prompt))
        def _():
            fs_ref[...] = _rms(acc_ref[...], fgain_ref[...])
    else:
        o_ref[...] = acc_ref[...]


def _out_ffn(x, y_prompt, y_sample, wo, gain, win, wout, fgain, final):
    nt = x.shape[0]
    tm = TOKEN_TILE
    npt, nst = y_prompt.shape[0] // tm, y_sample.shape[0] // tm
    row = lambda i: (i, 0)
    fixed = lambda i: (0, 0)
    prompt_row = lambda i: (jnp.minimum(i, npt - 1), 0)
    sample_row = lambda i: (jnp.maximum(i - npt, 0), 0)
    tile = lambda index_map: pl.BlockSpec((tm, D_MODEL), index_map)
    if final:
        out_specs = [tile(prompt_row), tile(sample_row)]
        out_shape = [jax.ShapeDtypeStruct((npt * tm, D_MODEL), F32), jax.ShapeDtypeStruct((nst * tm, D_MODEL), F32)]
    else:
        out_specs = [tile(row)]
        out_shape = [jax.ShapeDtypeStruct((nt, D_MODEL), F32)]
    return pl.pallas_call(
        functools.partial(_ffn_body, n_prompt_tiles=npt, final=final),
        grid=(nt // tm,),
        in_specs=[tile(row), tile(prompt_row), tile(sample_row),
                  pl.BlockSpec((D_MODEL, D_MODEL), fixed),
                  pl.BlockSpec((1, D_MODEL), fixed),
                  pl.BlockSpec((D_MODEL, 2 * D_FF), fixed),
                  pl.BlockSpec((D_FF, D_MODEL), fixed),
                  pl.BlockSpec((1, D_MODEL), fixed)],
        out_specs=out_specs,
        out_shape=out_shape,
        scratch_shapes=[pltpu.VMEM((tm, D_MODEL), F32)],
        compiler_params=pltpu.CompilerParams(
            dimension_semantics=("arbitrary",), vmem_limit_bytes=VMEM_LIMIT),
        name="out_ffn",
    )(x, y_prompt, y_sample, wo, gain, win, wout, fgain)


def _a_proj_body(x_ref, xp_ref, sh_ref, gain_ref, mu_ref, w0_ref, a0_ref,
                 wr_ref, wk_ref, wv_ref, w1_ref, a1_ref, g1_ref, w2_ref, a2_ref, g2_ref,
                 r_ref, k_ref, v_ref, lw_ref, al_ref, gg_ref, hl_ref, *, tm, npch, cps, cpd):
    i = pl.program_id(0)
    gain = gain_ref[...]
    h = _rms(x_ref[...], gain)
    prev_tile_last = _rms(xp_ref[...], gain)[7:8, :]
    row = _iota((CHUNK, 1), 0)
    pieces = []
    for c in range(tm // CHUNK):
        hc = h[_chunk_rows(c)]
        natural = prev_tile_last if c == 0 else h[c * CHUNK - 1:c * CHUNK]
        gc = i * (tm // CHUNK) + c
        start = jnp.where(gc < npch, gc % cps == 0, (gc - npch) % cpd == 0)
        first = jnp.where(start, sh_ref[c:c + 1, :], natural)
        pieces.append(jnp.where(row == 0, first, pltpu.roll(hc, 1, 0)))
        hl_ref[c:c + 1, :] = hc[CHUNK - 1:CHUNK]
    xx = jnp.concatenate(pieces, axis=0) - h
    mix = lambda j: h + xx * mu_ref[j:j + 1, :]
    r_ref[...] = _mm(mix(0), wr_ref[...])
    log_rate = -_softplus(-(w0_ref[...] + _mm(jnp.tanh(_mm(mix(1), w1_ref[...])), w2_ref[...]))) - 0.5
    lw_ref[...] = -jnp.exp(log_rate)
    k_ref[...] = _mm(mix(2), wk_ref[...])
    v_ref[...] = _mm(mix(3), wv_ref[...])
    al_ref[...] = _sigmoid(a0_ref[...] + _mm(_mm(mix(4), a1_ref[...]), a2_ref[...]))
    gg_ref[...] = _mm(_sigmoid(_mm(mix(5), g1_ref[...])), g2_ref[...])


def _a_proj(x, shift_rows, gain, mu, w0, a0, wr, wk, wv, w1, a1, g1, w2, a2, g2, npch, cps, cpd):
    nt = x.shape[0]
    tm = TOKEN_TILE
    row = lambda i: (i, 0)
    fixed = lambda i: (0, 0)
    full = lambda a: pl.BlockSpec(a.shape, fixed)
    tok = pl.BlockSpec((tm, D_MODEL), row)
    tok_shape = jax.ShapeDtypeStruct((nt, D_MODEL), F32)
    return pl.pallas_call(
        functools.partial(_a_proj_body, tm=tm, npch=npch, cps=cps, cpd=cpd),
        grid=(nt // tm,),
        in_specs=[tok,
                  pl.BlockSpec((8, D_MODEL), lambda i: (jnp.maximum(i * (tm // 8) - 1, 0), 0)),
                  pl.BlockSpec((tm // CHUNK, D_MODEL), row),
                  full(gain), full(mu), full(w0), full(a0), full(wr), full(wk), full(wv),
                  full(w1), full(a1), full(g1), full(w2), full(a2), full(g2)],
        out_specs=[tok] * 6 + [pl.BlockSpec((tm // CHUNK, D_MODEL), row)],
        out_shape=[tok_shape] * 6 + [jax.ShapeDtypeStruct((nt // CHUNK, D_MODEL), F32)],
        compiler_params=pltpu.CompilerParams(
            dimension_semantics=("arbitrary",), vmem_limit_bytes=VMEM_LIMIT),
        name="rwkv_proj",
    )(x, x, shift_rows, gain, mu, w0, a0, wr, wk, wv, w1, a1, g1, w2, a2, g2)


def _a_chunk_body(r_ref, k_ref, v_ref, lw_ref, al_ref, gg_ref, kk_ref, ka_ref, rk_ref, lnw_ref, lnb_ref,
                  *rest, nchunks, carry):
    if carry:
        y_ref, sout_ref, ss_ref = rest
    else:
        sin_ref, y_ref, sout_ref = rest

    lane = _iota((1, 128), 1)
    head0 = lane < A_HEAD
    ri = _iota((128, 128), 0) % CHUNK
    ci = _iota((128, 128), 1) % CHUNK
    strict = ri > ci
    incl = ri >= ci
    tri = (_iota((CHUNK, CHUNK), 0) >= _iota((CHUNK, CHUNK), 1)).astype(BF16)
    zero = jnp.zeros((A_HEAD, A_HEAD), F32)
    same_head = (_iota((128, 128), 0) // A_HEAD) == (_iota((128, 128), 1) // A_HEAD)
    head_mean = jnp.where(same_head, 1.0 / A_HEAD, 0.0).astype(BF16)

    def headsum(x):
        s0 = jnp.sum(jnp.where(head0, x, 0.0), axis=-1, keepdims=True)
        s1 = jnp.sum(jnp.where(head0, 0.0, x), axis=-1, keepdims=True)
        return jnp.where(head0, s0, s1)

    def expand(x):
        return jnp.concatenate([jnp.where(head0, x, 0.0), jnp.where(head0, 0.0, x)], axis=0)

    def load_state(ref, j):
        return [jnp.concatenate([jnp.concatenate([ref[j, 2 * p], zero], axis=1),
                                 jnp.concatenate([zero, ref[j, 2 * p + 1]], axis=1)], axis=0)
                for p in range(A_PAIRS)]

    def store_state(ref, j, state):
        for p in range(A_PAIRS):
            ref[j, 2 * p] = state[p][:A_HEAD, :A_HEAD]
            ref[j, 2 * p + 1] = state[p][A_HEAD:, A_HEAD:]

    pairs = range(A_PAIRS)
    sls = [slice(p * 128, (p + 1) * 128) for p in pairs]
    units = [(j, p) for j in range(nchunks) for p in pairs]
    gcum_all = [_exact_left(tri, lw_ref[_chunk_rows(j), :]) for j in range(nchunks)]
    xs, ys, ves, ends, glasts, bonus_sums = [], [], [], [], [], []
    for j, p in units:
        rs, sl = _chunk_rows(j), sls[p]
        k = k_ref[rs, sl]
        lw = lw_ref[rs, sl]
        al = al_ref[rs, sl]
        kk = k * kk_ref[:, sl]
        kk = kk * lax.rsqrt(jnp.maximum(headsum(kk * kk), 1e-24))
        kmod = k * (1.0 + (al - 1.0) * ka_ref[:, sl])
        gcum = gcum_all[j][:, sl]
        glast = gcum[CHUNK - 1:CHUNK, :]
        dec_out = jnp.exp(-gcum)
        dec_end = jnp.exp(glast - gcum)
        ae = expand(-kk * jnp.exp(gcum - lw))
        re = expand(r_ref[rs, sl] * jnp.exp(gcum))
        be = expand(kk * al * dec_out)
        ke = expand(kmod * dec_out)
        xs.append(jnp.concatenate([ae, re], axis=0).astype(BF16))
        ys.append(jnp.concatenate([be, ke], axis=0).astype(BF16))
        ves.append(expand(v_ref[rs, sl]).astype(BF16))
        ends.append(jnp.concatenate([expand(kk * al * dec_end), expand(kmod * dec_end)], axis=0).astype(BF16))
        glasts.append(glast)
        bonus_sums.append(headsum(r_ref[rs, sl] * kmod * rk_ref[:, sl]))

    sc = [_mm_nt(x, y) for x, y in zip(xs, ys)]
    n_ab = [jnp.where(strict, s[:128, :128], 0.0) for s in sc]
    l_ak = [jnp.where(strict, s[:128, 128:], 0.0).astype(BF16) for s in sc]
    m_rb = [jnp.where(incl, s[128:, :128], 0.0).astype(BF16) for s in sc]
    m_rk = [jnp.where(incl, s[128:, 128:], 0.0).astype(BF16) for s in sc]
    tinv = _nilpotent_inverse(n_ab)
    lv = [_mm(l, ve).astype(BF16) for l, ve in zip(l_ak, ves)]
    pq = [_mm(t, jnp.concatenate([x[:128], l], axis=1)).astype(BF16)
          for t, x, l in zip(tinv, xs, lv)]
    mt = [_mm(m, z) for m, z in zip(m_rb, pq)]
    mv = [_mm(m, ve) for m, ve in zip(m_rk, ves)]

    if carry:
        @pl.when(_is_first_step())
        def _():
            ss_ref[...] = jnp.zeros_like(ss_ref)

        state = [ss_ref[p] for p in pairs]
    for j in range(nchunks):
        if not carry:
            state = load_state(sin_ref, j)
        un = [j * A_PAIRS + p for p in pairs]
        lhs = [jnp.concatenate([(xs[u][128:] + mt[u][:, :128]).astype(BF16), pq[u][:, :128]], axis=0) for u in un]
        ou = [_mm_nt(a, s) for a, s in zip(lhs, state)]
        oe = [o[:128] + mt[u][:, 128:] + mv[u] for o, u in zip(ou, un)]
        uv = [jnp.concatenate([(o[128:] + pq[u][:, 128:]).astype(BF16), ves[u]], axis=0) for o, u in zip(ou, un)]
        upd = [_mm_tn(a, ends[u]) for a, u in zip(uv, un)]
        state = [s * jnp.exp(glasts[u]) + d for s, u, d in zip(state, un, upd)]
        if not carry:
            store_state(sout_ref, j, state)
        rs = _chunk_rows(j)
        os = [o[:CHUNK] + o[CHUNK:] for o in oe]
        parts = []
        for o in os:
            m = jnp.concatenate([o, o * o], axis=0)
            hi = m.astype(BF16)
            parts.append(jnp.concatenate([hi, (m - hi.astype(F32)).astype(BF16)], axis=0))
        mom = [jnp.dot(m, head_mean, preferred_element_type=F32) for m in parts]
        for p, sl in enumerate(sls):
            mean = mom[p][:CHUNK] + mom[p][2 * CHUNK:3 * CHUNK]
            var = mom[p][CHUNK:2 * CHUNK] + mom[p][3 * CHUNK:] - mean * mean
            on = (os[p] - mean) * lax.rsqrt(var + A_LNX_EPS) * lnw_ref[:, sl] + lnb_ref[:, sl]
            y_ref[rs, sl] = (on + bonus_sums[un[p]] * v_ref[rs, sl]) * gg_ref[rs, sl]
    if carry:
        for p in pairs:
            ss_ref[p] = state[p]

        @pl.when(_is_last_step())
        def _():
            store_state(sout_ref, 0, [ss_ref[p] for p in pairs])


def _chunk_calls(body, name, prompt_in, sample_in, shared, state_in, state_shapes, y_width, scratch,
                 bp, seq, bd, dseq):
    assert dseq == CHUNK, "sample streams are processed as single chunks"
    k = CHUNKS_PER_STEP
    rows = k * CHUNK
    steps = seq // rows
    params = pltpu.CompilerParams(dimension_semantics=("arbitrary",) * 2, vmem_limit_bytes=VMEM_LIMIT)

    def run(carry, grid, tok_index, col_specs, states, out_rows, nstate, state_index):
        ndim = len(grid)
        fixed = lambda a: pl.BlockSpec(a.shape, lambda *g: (0,) * a.ndim)
        tok_specs = [pl.BlockSpec((rows, w), functools.partial(lambda c, *g: (tok_index(*g), c), c))
                     for (_, w, c) in col_specs]
        st_specs = [pl.BlockSpec((nstate,) + s, functools.partial(lambda n, *g: (state_index(*g),) + (0,) * n, len(s)))
                    for s in state_shapes]
        y_spec = pl.BlockSpec((rows, y_width), lambda *g: (tok_index(*g) - tok_index(*(0,) * ndim), 0))
        return pl.pallas_call(
            functools.partial(body, nchunks=k, carry=carry),
            grid=grid,
            in_specs=tok_specs + [fixed(a) for a in shared] + (st_specs if states else []),
            out_specs=[y_spec] + st_specs,
            out_shape=[jax.ShapeDtypeStruct((out_rows, y_width), F32)]
            + [jax.ShapeDtypeStruct((grid[0] * nstate if not carry else grid[0],) + s, F32) for s in state_shapes],
            scratch_shapes=scratch if carry else [],
            compiler_params=params if ndim == 2 else pltpu.CompilerParams(
                dimension_semantics=("arbitrary",), vmem_limit_bytes=VMEM_LIMIT),
            name=name + ("_prompt" if carry else "_sample"),
        )(*[a for (a, _, _) in col_specs], *shared, *states)

    prompt = run(True, (bp, steps), lambda b, i: b * steps + i, prompt_in, [], bp * seq, 1, lambda b, i: b)
    off = bp * seq // rows
    sample = run(False, (bd // k,), lambda i: off + i, sample_in, state_in, bd * dseq, k, lambda i: i)
    return prompt, sample


def _head_rms_gate(o, gain, gate):
    return o * lax.rsqrt(jnp.mean(o * o, axis=-1, keepdims=True) + RMS_EPS) * gain * _silu(gate)


def _b_chunk_body(q_ref, k_ref, v_ref, gate_ref, lo_ref, wa2_ref, ba_ref, onorm_ref, *rest, nchunks, carry):
    if carry:
        y_ref, sout_ref, ss_ref = rest
    else:
        sin_ref, y_ref, sout_ref = rest
    incl = _iota((CHUNK, CHUNK), 0) >= _iota((CHUNK, CHUNK), 1)
    tri = incl.astype(BF16)
    heads = range(B_HEADS)
    kss = [slice(h * B_DK, (h + 1) * B_DK) for h in heads]
    vss = [slice(h * B_DV, (h + 1) * B_DV) for h in heads]
    units = [(j, h) for j in range(nchunks) for h in heads]
    gk_all = -_softplus(-(_mm(lo_ref[...], wa2_ref[...]) + ba_ref[...])) * (1.0 / B_GATE_TAU)
    b_all = [_exact_left(tri, gk_all[_chunk_rows(j)]) for j in range(nchunks)]
    qt, kt, khat, blast = [], [], [], []
    for j, h in units:
        rs, ks = _chunk_rows(j), kss[h]
        b = b_all[j][:, ks]
        k = k_ref[rs, ks]
        blast.append(b[CHUNK - 1:CHUNK, :])
        qt.append(q_ref[rs, ks] * (B_DK ** -0.5) * jnp.exp(b))
        kt.append(k * jnp.exp(-b))
        khat.append(k * jnp.exp(blast[-1] - b))
    a = [jnp.where(incl, _mm_nt(x, y), 0.0) for x, y in zip(qt, kt)]
    o_intra = [_mm(a[u], v_ref[_chunk_rows(j), vss[h]]) for u, (j, h) in enumerate(units)]
    upd = [_mm_tn(v_ref[_chunk_rows(j), vss[h]], khat[u]) for u, (j, h) in enumerate(units)]

    if carry:
        @pl.when(_is_first_step())
        def _():
            ss_ref[...] = jnp.zeros_like(ss_ref)

        state = [ss_ref[h] for h in heads]
    for j in range(nchunks):
        if not carry:
            state = [sin_ref[j, h].T for h in heads]
        rs = _chunk_rows(j)
        o = [o_intra[j * B_HEADS + h] + _mm_nt(qt[j * B_HEADS + h], state[h]) for h in heads]
        state = [state[h] * jnp.exp(blast[j * B_HEADS + h]) + upd[j * B_HEADS + h] for h in heads]
        for h in heads:
            if not carry:
                sout_ref[j, h] = state[h].T
            y_ref[rs, vss[h]] = _head_rms_gate(o[h], onorm_ref[...], gate_ref[rs, vss[h]])
    if carry:
        for h in heads:
            ss_ref[h] = state[h]

        @pl.when(_is_last_step())
        def _():
            for h in heads:
                sout_ref[0, h] = ss_ref[h].T


def _c_chunk_body(qkv_ref, z_ref, ba_ref, cw_ref, alog_ref, dtb_ref, onorm_ref, *rest, nchunks, carry):
    if carry:
        y_ref, cvout_ref, sout_ref, pv_ref, ss_ref = rest
    else:
        cvin_ref, sin_ref, y_ref, cvout_ref, sout_ref = rest

    def conv_act(x, pv):
        row8 = _iota((8, 1), 0)
        conv = x * cw_ref[C_CONV - 1:C_CONV, :]
        for j in range(1, C_CONV):
            sh = pltpu.roll(x, j, 0)
            head = jnp.where(row8 < j, pltpu.roll(pv, j, 0), sh[:8])
            sh = jnp.concatenate([head, sh[8:]], axis=0)
            conv = conv + sh * cw_ref[C_CONV - 1 - j:C_CONV - j, :]
        return _silu(conv)

    if carry:
        @pl.when(_is_first_step())
        def _():
            ss_ref[...] = jnp.zeros_like(ss_ref)
            pv_ref[...] = jnp.zeros_like(pv_ref)

        x = qkv_ref[...]
        act = conv_act(x, pv_ref[...])
        pv_ref[...] = x[nchunks * CHUNK - 8:]
    else:
        acts = []
        for j in range(nchunks):
            x = qkv_ref[_chunk_rows(j), :]
            acts.append(conv_act(x, cvin_ref[j]))
            cvout_ref[j] = x[CHUNK - 8:]
        act = jnp.concatenate(acts, axis=0)

    ba = ba_ref[...]
    beta_all = _sigmoid(ba)
    g_all = -jnp.exp(alog_ref[...]) * _softplus(ba + dtb_ref[...])
    tri = (_iota((CHUNK, CHUNK), 0) >= _iota((CHUNK, CHUNK), 1)).astype(BF16)
    triu2 = (_iota((CHUNK, 128), 0) <= _iota((CHUNK, 128), 1) % CHUNK).astype(BF16)
    ri = _iota((128, 128), 0)
    ci = _iota((128, 128), 1)
    same = (ri // CHUNK) == (ci // CHUNK)
    strict = same & (ri % CHUNK > ci % CHUNK)
    incl = same & (ri % CHUNK >= ci % CHUNK)
    lane_lo = _iota((1, 128), 1) < CHUNK
    zeros = jnp.zeros((CHUNK, C_DK), BF16)
    l2 = lambda t: t * lax.rsqrt(jnp.sum(t * t, axis=-1, keepdims=True) + 1e-6)
    blockdiag = lambda a, b: jnp.concatenate(
        [jnp.concatenate([a.astype(BF16), zeros], axis=1),
         jnp.concatenate([zeros, b.astype(BF16)], axis=1)], axis=0)

    heads = range(C_HEADS)
    npairs = C_HEADS // 2
    q, k, v, beta, kb, col, egc, glast = ({} for _ in range(8))
    gc_row = []
    for j in range(nchunks):
        rs = _chunk_rows(j)
        gc_col = _exact_left(tri, g_all[rs])
        gc_row.append(_exact_tn(g_all[rs], triu2))
        for h in heads:
            u = (j, h)
            q[u] = l2(act[rs, h * C_DK:(h + 1) * C_DK]) * (C_DK ** -0.5)
            k[u] = l2(act[rs, C_HEADS * C_DK + h * C_DK:C_HEADS * C_DK + (h + 1) * C_DK])
            v[u] = act[rs, 2 * C_HEADS * C_DK + h * C_DV:2 * C_HEADS * C_DK + (h + 1) * C_DV]
            beta[u] = beta_all[rs, h:h + 1]
            kb[u] = k[u] * beta[u]
            col[u] = gc_col[:, C_HEADS + h:C_HEADS + h + 1]
            egc[u] = jnp.exp(col[u])
            glast[u] = col[u][CHUNK - 1:CHUNK, :]
    units = [(j, p) for j in range(nchunks) for p in range(npairs)]
    decay, sc, rhs = [], [], []
    for j, p in units:
        a, b = (j, 2 * p), (j, 2 * p + 1)
        col_pair = jnp.concatenate([col[a], col[b]], axis=0)
        row_pair = jnp.where(lane_lo, gc_row[j][C_HEADS + 2 * p:C_HEADS + 2 * p + 1, :],
                             gc_row[j][C_HEADS + 2 * p + 1:C_HEADS + 2 * p + 2, :])
        decay.append(jnp.where(incl, jnp.exp(jnp.where(incl, col_pair - row_pair, 0.0)), 0.0))
        lhs = jnp.concatenate([blockdiag(kb[a], kb[b]), blockdiag(q[a], q[b])], axis=0)
        sc.append(_mm_nt(lhs, blockdiag(k[a], k[b])))
        rhs.append(jnp.concatenate(
            [jnp.concatenate([v[u] * beta[u], kb[u] * egc[u]], axis=1) for u in (a, b)], axis=0).astype(BF16))
    n_l = [-jnp.where(strict, s[:128], 0.0) * d for s, d in zip(sc, decay)]
    attn = [(s[128:] * d).astype(BF16) for s, d in zip(sc, decay)]
    tinv = _nilpotent_inverse(n_l)
    sol = [_mm(t, r) for t, r in zip(tinv, rhs)]

    if carry:
        state = [ss_ref[h] for h in heads]
    for j in range(nchunks):
        if not carry:
            state = [sin_ref[j, h] for h in heads]
        rs = _chunk_rows(j)
        half = lambda m, h: m[(h % 2) * CHUNK:(h % 2 + 1) * CHUNK]
        qw = [_mm(jnp.concatenate([q[j, h] * egc[j, h], half(sol[j * npairs + h // 2], h)[:, C_DV:]], axis=0),
                  state[h]) for h in heads]
        v_new = [(sol[j * npairs + p][:, :C_DV]
                  - jnp.concatenate([qw[2 * p][CHUNK:], qw[2 * p + 1][CHUNK:]], axis=0)).astype(BF16)
                 for p in range(npairs)]
        o_intra = [_mm(attn[j * npairs + p], v_new[p]) for p in range(npairs)]
        upd = [_mm_tn(k[j, h] * jnp.exp(glast[j, h] - col[j, h]), half(v_new[h // 2], h)) for h in heads]
        state = [state[h] * jnp.exp(glast[j, h]) + upd[h] for h in heads]
        for h in heads:
            o = qw[h][:CHUNK] + half(o_intra[h // 2], h)
            vs = slice(h * C_DV, (h + 1) * C_DV)
            y_ref[rs, vs] = _head_rms_gate(o, onorm_ref[...], z_ref[rs, vs])
            if not carry:
                sout_ref[j, h] = state[h]
    if carry:
        for h in heads:
            ss_ref[h] = state[h]

        @pl.when(_is_last_step())
        def _():
            sout_ref[0] = ss_ref[...]
            cvout_ref[0] = pv_ref[...]


def _pad_cols(w, n):
    return jnp.pad(w, ((0, 0), (0, n - w.shape[1])))


def kernel(x_prompt, x_sample, state_a_shift, state_a_wkv, state_b_kv, state_c_conv, state_c_kv, norm_mix, norm_ffn, norm_final, ffn_w_in, ffn_w_out, a_mu, a_w0, a_w1, a_w2, a_a0, a_a1, a_a2, a_g1, a_g2, a_k_k, a_k_a, a_r_k, a_w_rkv, a_w_o, a_lnx_w, a_lnx_b, b_w_in, b_w_a1, b_w_a2, b_b_a, b_onorm, b_w_o, c_w_in, c_conv_w, c_a_log, c_dt_bias, c_onorm, c_w_o):
    bp, seq, d = x_prompt.shape
    bd, dseq, _ = x_sample.shape
    depth = norm_mix.shape[0]
    cps, cpd = seq // CHUNK, dseq // CHUNK
    npch = bp * cps
    shape = (bp, seq, bd, dseq)
    x = jnp.concatenate([x_prompt.reshape(bp * seq, d), x_sample.reshape(bd * dseq, d)], axis=0)
    row = lambda a: a.reshape(1, -1)
    bf = lambda a: a.astype(BF16)

    last_prompt = [(s + 1) * cps - 1 for s in range(bp)]
    last_sample = [npch + (s + 1) * cpd - 1 for s in range(bd)]

    out_a_shift, out_a_wkv, out_b_kv, out_c_conv, out_c_kv = [], [], [], [], []
    for i in range(depth):
        j = i // 3
        gain = row(norm_mix[i])
        if i % 3 == 0:
            shift_rows = jnp.zeros((x.shape[0] // CHUNK, d), F32)
            shift_rows = shift_rows.at[jnp.asarray([npch + s * cpd for s in range(bd)])].set(state_a_shift[j])
            proj = _a_proj(
                x, shift_rows, gain, a_mu[j], row(a_w0[j]), row(a_a0[j]),
                bf(a_w_rkv[j, 0]), bf(a_w_rkv[j, 1]), bf(a_w_rkv[j, 2]),
                bf(a_w1[j]), bf(a_a1[j]), bf(a_g1[j]), bf(a_w2[j]), bf(a_a2[j]), bf(a_g2[j]), npch, cps, cpd)
            hl = proj[6]
            cols = [(a, D_MODEL, 0) for a in proj[:6]]
            shared = [row(a_k_k[j]), row(a_k_a[j]), row(a_r_k[j]), row(a_lnx_w[j]), row(a_lnx_b[j])]
            (y_p, wkv_p), (y_s, wkv_s) = _chunk_calls(
                _a_chunk_body, "rwkv_chunk", cols, cols, shared, [state_a_wkv[j]],
                [(A_HEADS, A_HEAD, A_HEAD)], D_MODEL, [pltpu.VMEM((A_PAIRS, 128, 128), F32)], *shape)
            out_a_shift.append((hl[jnp.asarray(last_prompt)], hl[jnp.asarray(last_sample)]))
            out_a_wkv.append((wkv_p, wkv_s))
            w_o = a_w_o[j]
        elif i % 3 == 1:
            w = jnp.concatenate([b_w_in[j], _pad_cols(b_w_a1[j], B_LORA_PAD)], axis=1)
            proj = _norm_proj(x, gain, bf(w), 640)
            wa2 = jnp.pad(b_w_a2[j], ((0, B_LORA_PAD - b_w_a2.shape[1]), (0, 0)))
            qk_w, v_w = B_HEADS * B_DK, B_HEADS * B_DV
            cols = [(proj, qk_w, 0), (proj, qk_w, 1), (proj, v_w, 1), (proj, v_w, 2),
                    (proj, B_LORA_PAD, (2 * qk_w + 2 * v_w) // B_LORA_PAD)]
            shared = [bf(wa2), row(b_b_a[j]), row(b_onorm[j])]
            (y_p, kv_p), (y_s, kv_s) = _chunk_calls(
                _b_chunk_body, "gla_chunk", cols, cols, shared, [state_b_kv[j]],
                [(B_HEADS, B_DK, B_DV)], v_w, [pltpu.VMEM((B_HEADS, B_DV, B_DK), F32)], *shape)
            out_b_kv.append((kv_p, kv_s))
            w_o = b_w_o[j]
        else:
            proj = _norm_proj(x, gain, bf(_pad_cols(c_w_in[j], C_PROJ_PAD)), 1408)
            lanes = lambda a: jnp.pad(a.reshape(1, -1), ((0, 0), (C_HEADS, 128 - 2 * C_HEADS)))
            z_w = C_HEADS * C_DV
            cols = [(proj, C_QKV, 0), (proj, z_w, C_QKV // z_w), (proj, 128, (C_QKV + z_w) // 128)]
            shared = [c_conv_w[j], lanes(c_a_log[j]), lanes(c_dt_bias[j]), row(c_onorm[j])]
            cv_in = jnp.pad(state_c_conv[j], ((0, 0), (8 - (C_CONV - 1), 0), (0, 0)))
            (y_p, cv_p, kv_p), (y_s, cv_s, kv_s) = _chunk_calls(
                _c_chunk_body, "gdn_chunk", cols, cols, shared, [cv_in, state_c_kv[j]],
                [(8, C_QKV), (C_HEADS, C_DK, C_DV)], z_w,
                [pltpu.VMEM((8, C_QKV), F32), pltpu.VMEM((C_HEADS, C_DK, C_DV), F32)], *shape)
            out_c_conv.append((cv_p[:, 8 - (C_CONV - 1):], cv_s[:, 8 - (C_CONV - 1):]))
            out_c_kv.append((kv_p, kv_s))
            w_o = c_w_o[j]
        final = i == depth - 1
        res = _out_ffn(x, y_p, y_s, bf(w_o), row(norm_ffn[i]), bf(ffn_w_in[i]), bf(ffn_w_out[i]),
                       row(norm_final), final)
        if not final:
            x = res[0]

    split = lambda parts: (jnp.stack([p for p, _ in parts]), jnp.stack([s for _, s in parts]))
    a_shift_p, a_shift_s = split(out_a_shift)
    a_wkv_p, a_wkv_s = split(out_a_wkv)
    b_kv_p, b_kv_s = split(out_b_kv)
    c_conv_p, c_conv_s = split(out_c_conv)
    c_kv_p, c_kv_s = split(out_c_kv)
    y_prompt = res[0].reshape(bp, seq, d)
    y_sample = res[1].reshape(bd, dseq, d)
    return (y_prompt, y_sample, a_shift_p, a_wkv_p, b_kv_p, c_conv_p, c_kv_p,
            a_shift_s, a_wkv_s, b_kv_s, c_conv_s, c_kv_s)
```

```python
import functools
import math

import jax
import jax.numpy as jnp
from jax import lax
from jax.experimental import pallas as pl
from jax.experimental.pallas import tpu as pltpu

F32 = jnp.float32
BF16 = jnp.bfloat16

D_MODEL = 1024
CHUNK = 64
RMS_EPS = 1e-6
D_FF = 2816
FFN_COLS = 256
PROJ_COLS = 512

A_HEADS = 16
A_HEAD = 64
A_PAIRS = A_HEADS // 2
A_LNX_EPS = 64e-5
DECAY_SCALE = math.exp(-0.5)

B_HEADS = 4
B_DK = 128
B_DV = 256
B_GATE_TAU = 16.0
B_LORA_PAD = 128

C_HEADS = 8
C_DK = 128
C_DV = 128
C_CONV = 4
C_QKV = C_HEADS * (2 * C_DK + C_DV)
C_PROJ_PAD = C_QKV + C_HEADS * C_DV + 128

TOKEN_TILE = 512
A_CHUNKS_PER_STEP = 2
B_CHUNKS_PER_STEP = 4
C_CHUNKS_PER_STEP = 2
VMEM_LIMIT = 56 * 1024 * 1024


def _mm(a, b):
    return jnp.dot(a.astype(BF16), b.astype(BF16), preferred_element_type=F32)


def _mm_nt(a, b):
    return lax.dot_general(a.astype(BF16), b.astype(BF16), (((1,), (1,)), ((), ())),
                           preferred_element_type=F32)


def _mm_tn(a, b):
    return lax.dot_general(a.astype(BF16), b.astype(BF16), (((0,), (0,)), ((), ())),
                           preferred_element_type=F32)


def _split3(x):
    hi = x.astype(BF16)
    r1 = x - hi.astype(F32)
    mid = r1.astype(BF16)
    lo = (r1 - mid.astype(F32)).astype(BF16)
    return hi, mid, lo


def _exact_left(m_bf16, x):
    hi, mid, lo = _split3(x)
    dot = lambda p: jnp.dot(m_bf16, p, preferred_element_type=F32)
    return dot(hi) + dot(mid) + dot(lo)


def _exact_tn(x, m_bf16):
    hi, mid, lo = _split3(x)
    dot = lambda p: lax.dot_general(p, m_bf16, (((0,), (0,)), ((), ())), preferred_element_type=F32)
    return dot(hi) + dot(mid) + dot(lo)


def _rms(x, gain):
    return x * lax.rsqrt(jnp.mean(x * x, axis=-1, keepdims=True) + RMS_EPS) * gain


def _softplus(x):
    return jnp.maximum(x, 0.0) + jnp.log1p(jnp.exp(-jnp.abs(x)))


def _sigmoid(x):
    return 1.0 / (1.0 + jnp.exp(-x))


def _silu(x):
    return x * _sigmoid(x)


def _iota(shape, dim):
    return lax.broadcasted_iota(jnp.int32, shape, dim)


def _nilpotent_inverse(ns):
    eye = (_iota((128, 128), 0) == _iota((128, 128), 1)).astype(F32)
    inv = [eye + n for n in ns]
    pw = [_mm(n, n).astype(BF16) for n in ns]
    for step in range(5):
        if step < 4:
            both = [_mm(jnp.concatenate([t.astype(BF16), m], axis=0), m) for t, m in zip(inv, pw)]
            inv = [t + b[:128] for t, b in zip(inv, both)]
            pw = [b[128:].astype(BF16) for b in both]
        else:
            inv = [t + _mm(t, m) for t, m in zip(inv, pw)]
    return [t.astype(BF16) for t in inv]


def _chunk_rows(j):
    return slice(j * CHUNK, (j + 1) * CHUNK)


def _is_first_step():
    return pl.program_id(1) == 0


def _is_last_step():
    return pl.program_id(1) == pl.num_programs(1) - 1


def _proj_body(x_ref, gain_ref, w_ref, o_ref, *, tm, n):
    h = _rms(x_ref[...], gain_ref[...]).astype(BF16)
    for lo in range(0, n, PROJ_COLS):
        cols = slice(lo, min(lo + PROJ_COLS, n))
        o_ref[:, cols] = jnp.dot(h, w_ref[:, cols], preferred_element_type=F32)


def _norm_proj(x, gain, w):
    nt, n = x.shape[0], w.shape[1]
    tm = TOKEN_TILE
    return pl.pallas_call(
        functools.partial(_proj_body, tm=tm, n=n),
        grid=(nt // tm,),
        in_specs=[pl.BlockSpec((tm, D_MODEL), lambda i: (i, 0)),
                  pl.BlockSpec((1, D_MODEL), lambda i: (0, 0)),
                  pl.BlockSpec((D_MODEL, n), lambda i: (0, 0))],
        out_specs=pl.BlockSpec((tm, n), lambda i: (i, 0)),
        out_shape=jax.ShapeDtypeStruct((nt, n), F32),
        compiler_params=pltpu.CompilerParams(
            dimension_semantics=("arbitrary",), vmem_limit_bytes=VMEM_LIMIT),
        name="norm_proj",
    )(x, gain, w)


def _ffn_body(x_ref, yp_ref, ys_ref, wo_ref, gain_ref, win_ref, wout_ref, fgain_ref, *rest, n_prompt_tiles, final):
    if final:
        fp_ref, fs_ref, acc_ref = rest
    else:
        o_ref, acc_ref = rest
    in_prompt = pl.program_id(0) < n_prompt_tiles
    y = jnp.where(in_prompt, yp_ref[...], ys_ref[...])
    x1 = x_ref[...] + jnp.dot(y.astype(BF16), wo_ref[...], preferred_element_type=F32)
    h = _rms(x1, gain_ref[...]).astype(BF16)
    acc_ref[...] = x1
    for c in range(D_FF // FFN_COLS):
        lo = c * FFN_COLS
        g = jnp.dot(h, win_ref[:, lo:lo + FFN_COLS], preferred_element_type=F32)
        u = jnp.dot(h, win_ref[:, D_FF + lo:D_FF + lo + FFN_COLS], preferred_element_type=F32)
        a = (_silu(g) * u).astype(BF16)
        acc_ref[...] += jnp.dot(a, wout_ref[lo:lo + FFN_COLS, :], preferred_element_type=F32)
    if final:
        @pl.when(in_prompt)
        def _():
            fp_ref[...] = _rms(acc_ref[...], fgain_ref[...])

        @pl.when(jnp.logical_not(in_prompt))
        def _():
            fs_ref[...] = _rms(acc_ref[...], fgain_ref[...])
    else:
        o_ref[...] = acc_ref[...]


def _out_ffn(x, y_prompt, y_sample, wo, gain, win, wout, fgain, final):
    nt = x.shape[0]
    tm = TOKEN_TILE
    npt, nst = y_prompt.shape[0] // tm, y_sample.shape[0] // tm
    row = lambda i: (i, 0)
    fixed = lambda i: (0, 0)
    prompt_row = lambda i: (jnp.minimum(i, npt - 1), 0)
    sample_row = lambda i: (jnp.maximum(i - npt, 0), 0)
    tile = lambda index_map: pl.BlockSpec((tm, D_MODEL), index_map)
    if final:
        out_specs = [tile(prompt_row), tile(sample_row)]
        out_shape = [jax.ShapeDtypeStruct((npt * tm, D_MODEL), F32), jax.ShapeDtypeStruct((nst * tm, D_MODEL), F32)]
    else:
        out_specs = [tile(row)]
        out_shape = [jax.ShapeDtypeStruct((nt, D_MODEL), F32)]
    return pl.pallas_call(
        functools.partial(_ffn_body, n_prompt_tiles=npt, final=final),
        grid=(nt // tm,),
        in_specs=[tile(row), tile(prompt_row), tile(sample_row),
                  pl.BlockSpec((D_MODEL, D_MODEL), fixed),
                  pl.BlockSpec((1, D_MODEL), fixed),
                  pl.BlockSpec((D_MODEL, 2 * D_FF), fixed),
                  pl.BlockSpec((D_FF, D_MODEL), fixed),
                  pl.BlockSpec((1, D_MODEL), fixed)],
        out_specs=out_specs,
        out_shape=out_shape,
        scratch_shapes=[pltpu.VMEM((tm, D_MODEL), F32)],
        compiler_params=pltpu.CompilerParams(
            dimension_semantics=("arbitrary",), vmem_limit_bytes=VMEM_LIMIT),
        name="out_ffn",
    )(x, y_prompt, y_sample, wo, gain, win, wout, fgain)


def _a_proj_body(x_ref, xp_ref, sh_ref, gain_ref, mu_ref, w0_ref, a0_ref,
                 wr_ref, wk_ref, wv_ref, w1_ref, a1_ref, g1_ref, w2_ref, a2_ref, g2_ref,
                 r_ref, k_ref, v_ref, lw_ref, al_ref, gg_ref, hl_ref, *, tm, npch, cps, cpd):
    i = pl.program_id(0)
    gain = gain_ref[...]
    h = _rms(x_ref[...], gain)
    prev_tile_last = _rms(xp_ref[...], gain)[7:8, :]
    row = _iota((CHUNK, 1), 0)
    pieces = []
    for c in range(tm // CHUNK):
        hc = h[_chunk_rows(c)]
        natural = prev_tile_last if c == 0 else h[c * CHUNK - 1:c * CHUNK]
        gc = i * (tm // CHUNK) + c
        start = jnp.where(gc < npch, gc % cps == 0, (gc - npch) % cpd == 0)
        first = jnp.where(start, sh_ref[c:c + 1, :], natural)
        pieces.append(jnp.where(row == 0, first, pltpu.roll(hc, 1, 0)))
        hl_ref[c:c + 1, :] = hc[CHUNK - 1:CHUNK]
    xx = jnp.concatenate(pieces, axis=0) - h
    mix = lambda j: (h + xx * mu_ref[j:j + 1, :]).astype(BF16)
    dot = lambda a, w_ref: jnp.dot(a, w_ref[...], preferred_element_type=F32)
    r_ref[...] = dot(mix(0), wr_ref)
    z = w0_ref[...] + _mm(jnp.tanh(dot(mix(1), w1_ref)), w2_ref[...])
    lw_ref[...] = (-DECAY_SCALE) * _sigmoid(z)
    k_ref[...] = dot(mix(2), wk_ref)
    v_ref[...] = dot(mix(3), wv_ref)
    al_ref[...] = _sigmoid(a0_ref[...] + _mm(dot(mix(4), a1_ref), a2_ref[...]))
    gg_ref[...] = _mm(_sigmoid(dot(mix(5), g1_ref)), g2_ref[...])


def _a_proj(x, shift_rows, gain, mu, w0, a0, wr, wk, wv, w1, a1, g1, w2, a2, g2, npch, cps, cpd):
    nt = x.shape[0]
    tm = TOKEN_TILE
    row = lambda i: (i, 0)
    fixed = lambda i: (0, 0)
    full = lambda a: pl.BlockSpec(a.shape, fixed)
    tok = pl.BlockSpec((tm, D_MODEL), row)
    tok_shape = jax.ShapeDtypeStruct((nt, D_MODEL), F32)
    return pl.pallas_call(
        functools.partial(_a_proj_body, tm=tm, npch=npch, cps=cps, cpd=cpd),
        grid=(nt // tm,),
        in_specs=[tok,
                  pl.BlockSpec((8, D_MODEL), lambda i: (jnp.maximum(i * (tm // 8) - 1, 0), 0)),
                  pl.BlockSpec((tm // CHUNK, D_MODEL), row),
                  full(gain), full(mu), full(w0), full(a0), full(wr), full(wk), full(wv),
                  full(w1), full(a1), full(g1), full(w2), full(a2), full(g2)],
        out_specs=[tok] * 6 + [pl.BlockSpec((tm // CHUNK, D_MODEL), row)],
        out_shape=[tok_shape] * 6 + [jax.ShapeDtypeStruct((nt // CHUNK, D_MODEL), F32)],
        compiler_params=pltpu.CompilerParams(
            dimension_semantics=("arbitrary",), vmem_limit_bytes=VMEM_LIMIT),
        name="rwkv_proj",
    )(x, x, shift_rows, gain, mu, w0, a0, wr, wk, wv, w1, a1, g1, w2, a2, g2)


def _a_chunk_body(r_ref, k_ref, v_ref, lw_ref, al_ref, gg_ref, kk_ref, ka_ref, rk_ref, lnw_ref, lnb_ref,
                  *rest, nchunks, carry):
    if carry:
        y_ref, sout_ref, ss_ref = rest
    else:
        sin_ref, y_ref, sout_ref = rest

    lane = _iota((1, 128), 1)
    head0 = lane < A_HEAD
    ri = _iota((128, 128), 0) % CHUNK
    ci = _iota((128, 128), 1) % CHUNK
    strict = ri > ci
    incl = ri >= ci
    tri = (_iota((CHUNK, CHUNK), 0) >= _iota((CHUNK, CHUNK), 1)).astype(BF16)
    zero = jnp.zeros((A_HEAD, A_HEAD), F32)
    same_head = (_iota((128, 128), 0) // A_HEAD) == (_iota((128, 128), 1) // A_HEAD)
    head_mean = jnp.where(same_head, 1.0 / A_HEAD, 0.0).astype(BF16)

    def headsum(x):
        s0 = jnp.sum(jnp.where(head0, x, 0.0), axis=-1, keepdims=True)
        s1 = jnp.sum(jnp.where(head0, 0.0, x), axis=-1, keepdims=True)
        return jnp.where(head0, s0, s1)

    def expand(x):
        return jnp.concatenate([jnp.where(head0, x, 0.0), jnp.where(head0, 0.0, x)], axis=0)

    def load_state(ref, j):
        return [jnp.concatenate([jnp.concatenate([ref[j, 2 * p], zero], axis=1),
                                 jnp.concatenate([zero, ref[j, 2 * p + 1]], axis=1)], axis=0)
                for p in range(A_PAIRS)]

    def store_state(ref, j, state):
        for p in range(A_PAIRS):
            ref[j, 2 * p] = state[p][:A_HEAD, :A_HEAD]
            ref[j, 2 * p + 1] = state[p][A_HEAD:, A_HEAD:]

    pairs = range(A_PAIRS)
    sls = [slice(p * 128, (p + 1) * 128) for p in pairs]
    units = [(j, p) for j in range(nchunks) for p in pairs]
    gcum_all = [_exact_left(tri, lw_ref[_chunk_rows(j), :]) for j in range(nchunks)]
    xs, ys, ves, ends, glasts, bonus_sums = [], [], [], [], [], []
    for j, p in units:
        rs, sl = _chunk_rows(j), sls[p]
        k = k_ref[rs, sl]
        lw = lw_ref[rs, sl]
        al = al_ref[rs, sl]
        kk = k * kk_ref[:, sl]
        kk = kk * lax.rsqrt(jnp.maximum(headsum(kk * kk), 1e-24))
        kmod = k * (1.0 + (al - 1.0) * ka_ref[:, sl])
        gcum = gcum_all[j][:, sl]
        glast = gcum[CHUNK - 1:CHUNK, :]
        dec_out = jnp.exp(-gcum)
        dec_end = jnp.exp(glast - gcum)
        ae = expand(-kk * jnp.exp(gcum - lw))
        re = expand(r_ref[rs, sl] * jnp.exp(gcum))
        be = expand(kk * al * dec_out)
        ke = expand(kmod * dec_out)
        xs.append(jnp.concatenate([ae, re], axis=0).astype(BF16))
        ys.append(jnp.concatenate([be, ke], axis=0).astype(BF16))
        ves.append(expand(v_ref[rs, sl]).astype(BF16))
        ends.append(jnp.concatenate([expand(kk * al * dec_end), expand(kmod * dec_end)], axis=0).astype(BF16))
        glasts.append(glast)
        bonus_sums.append(headsum(r_ref[rs, sl] * kmod * rk_ref[:, sl]))

    sc = [_mm_nt(x, y) for x, y in zip(xs, ys)]
    n_ab = [jnp.where(strict, s[:128, :128], 0.0) for s in sc]
    l_ak = [jnp.where(strict, s[:128, 128:], 0.0).astype(BF16) for s in sc]
    m_rb = [jnp.where(incl, s[128:, :128], 0.0).astype(BF16) for s in sc]
    m_rk = [jnp.where(incl, s[128:, 128:], 0.0).astype(BF16) for s in sc]
    tinv = _nilpotent_inverse(n_ab)
    lv = [_mm(l, ve).astype(BF16) for l, ve in zip(l_ak, ves)]
    pq = [_mm(t, jnp.concatenate([x[:128], l], axis=1)).astype(BF16)
          for t, x, l in zip(tinv, xs, lv)]
    mt = [_mm(m, z) for m, z in zip(m_rb, pq)]
    mv = [_mm(m, ve) for m, ve in zip(m_rk, ves)]

    if carry:
        @pl.when(_is_first_step())
        def _():
            ss_ref[...] = jnp.zeros_like(ss_ref)

        state = [ss_ref[p] for p in pairs]
    for j in range(nchunks):
        if not carry:
            state = load_state(sin_ref, j)
        un = [j * A_PAIRS + p for p in pairs]
        lhs = [jnp.concatenate([(xs[u][128:] + mt[u][:, :128]).astype(BF16), pq[u][:, :128]], axis=0) for u in un]
        ou = [_mm_nt(a, s) for a, s in zip(lhs, state)]
        oe = [o[:128] + mt[u][:, 128:] + mv[u] for o, u in zip(ou, un)]
        uv = [jnp.concatenate([(o[128:] + pq[u][:, 128:]).astype(BF16), ves[u]], axis=0) for o, u in zip(ou, un)]
        upd = [_mm_tn(a, ends[u]) for a, u in zip(uv, un)]
        state = [s * jnp.exp(glasts[u]) + d for s, u, d in zip(state, un, upd)]
        if not carry:
            store_state(sout_ref, j, state)
        rs = _chunk_rows(j)
        os = [o[:CHUNK] + o[CHUNK:] for o in oe]
        parts = []
        for o in os:
            m = jnp.concatenate([o, o * o], axis=0)
            hi = m.astype(BF16)
            parts.append(jnp.concatenate([hi, (m - hi.astype(F32)).astype(BF16)], axis=0))
        mom = [jnp.dot(m, head_mean, preferred_element_type=F32) for m in parts]
        for p, sl in enumerate(sls):
            mean = mom[p][:CHUNK] + mom[p][2 * CHUNK:3 * CHUNK]
            var = mom[p][CHUNK:2 * CHUNK] + mom[p][3 * CHUNK:] - mean * mean
            on = (os[p] - mean) * lax.rsqrt(var + A_LNX_EPS) * lnw_ref[:, sl] + lnb_ref[:, sl]
            y_ref[rs, sl] = (on + bonus_sums[un[p]] * v_ref[rs, sl]) * gg_ref[rs, sl]
    if carry:
        for p in pairs:
            ss_ref[p] = state[p]

        @pl.when(_is_last_step())
        def _():
            store_state(sout_ref, 0, [ss_ref[p] for p in pairs])


def _chunk_calls(body, name, k, prompt_in, sample_in, shared, state_in, state_shapes, y_width, scratch,
                 bp, seq, bd, dseq):
    assert dseq == CHUNK, "sample streams are processed as single chunks"
    rows = k * CHUNK
    steps = seq // rows
    params = pltpu.CompilerParams(dimension_semantics=("arbitrary",) * 2, vmem_limit_bytes=VMEM_LIMIT)

    def run(carry, grid, tok_index, col_specs, states, out_rows, nstate, state_index):
        ndim = len(grid)
        fixed = lambda a: pl.BlockSpec(a.shape, lambda *g: (0,) * a.ndim)
        tok_specs = [pl.BlockSpec((rows, w), functools.partial(lambda c, *g: (tok_index(*g), c), c))
                     for (_, w, c) in col_specs]
        st_specs = [pl.BlockSpec((nstate,) + s, functools.partial(lambda n, *g: (state_index(*g),) + (0,) * n, len(s)))
                    for s in state_shapes]
        y_spec = pl.BlockSpec((rows, y_width), lambda *g: (tok_index(*g) - tok_index(*(0,) * ndim), 0))
        return pl.pallas_call(
            functools.partial(body, nchunks=k, carry=carry),
            grid=grid,
            in_specs=tok_specs + [fixed(a) for a in shared] + (st_specs if states else []),
            out_specs=[y_spec] + st_specs,
            out_shape=[jax.ShapeDtypeStruct((out_rows, y_width), F32)]
            + [jax.ShapeDtypeStruct((grid[0] * nstate if not carry else grid[0],) + s, F32) for s in state_shapes],
            scratch_shapes=scratch if carry else [],
            compiler_params=params if ndim == 2 else pltpu.CompilerParams(
                dimension_semantics=("arbitrary",), vmem_limit_bytes=VMEM_LIMIT),
            name=name + ("_prompt" if carry else "_sample"),
        )(*[a for (a, _, _) in col_specs], *shared, *states)

    prompt = run(True, (bp, steps), lambda b, i: b * steps + i, prompt_in, [], bp * seq, 1, lambda b, i: b)
    off = bp * seq // rows
    sample = run(False, (bd // k,), lambda i: off + i, sample_in, state_in, bd * dseq, k, lambda i: i)
    return prompt, sample


def _head_rms_gate(o, gain, gate):
    return o * lax.rsqrt(jnp.mean(o * o, axis=-1, keepdims=True) + RMS_EPS) * gain * _silu(gate)


def _b_chunk_body(q_ref, k_ref, v_ref, gate_ref, lo_ref, wa2_ref, ba_ref, onorm_ref, *rest, nchunks, carry):
    if carry:
        y_ref, sout_ref, ss_ref = rest
    else:
        sin_ref, y_ref, sout_ref = rest
    incl = _iota((CHUNK, CHUNK), 0) >= _iota((CHUNK, CHUNK), 1)
    tri = incl.astype(BF16)
    heads = range(B_HEADS)
    kss = [slice(h * B_DK, (h + 1) * B_DK) for h in heads]
    vss = [slice(h * B_DV, (h + 1) * B_DV) for h in heads]
    units = [(j, h) for j in range(nchunks) for h in heads]
    gk_all = -_softplus(-(_mm(lo_ref[...], wa2_ref[...]) + ba_ref[...])) * (1.0 / B_GATE_TAU)
    b_all = [_exact_left(tri, gk_all[_chunk_rows(j)]) for j in range(nchunks)]
    qt, kt, khat, blast = [], [], [], []
    for j, h in units:
        rs, ks = _chunk_rows(j), kss[h]
        b = b_all[j][:, ks]
        k = k_ref[rs, ks]
        blast.append(b[CHUNK - 1:CHUNK, :])
        qt.append(q_ref[rs, ks] * (B_DK ** -0.5) * jnp.exp(b))
        kt.append(k * jnp.exp(-b))
        khat.append(k * jnp.exp(blast[-1] - b))
    a = [jnp.where(incl, _mm_nt(x, y), 0.0) for x, y in zip(qt, kt)]
    o_intra = [_mm(a[u], v_ref[_chunk_rows(j), vss[h]]) for u, (j, h) in enumerate(units)]
    upd = [_mm_tn(v_ref[_chunk_rows(j), vss[h]], khat[u]) for u, (j, h) in enumerate(units)]

    if carry:
        @pl.when(_is_first_step())
        def _():
            ss_ref[...] = jnp.zeros_like(ss_ref)

        state = [ss_ref[h] for h in heads]
    for j in range(nchunks):
        if not carry:
            state = [sin_ref[j, h].T for h in heads]
        rs = _chunk_rows(j)
        o = [o_intra[j * B_HEADS + h] + _mm_nt(qt[j * B_HEADS + h], state[h]) for h in heads]
        state = [state[h] * jnp.exp(blast[j * B_HEADS + h]) + upd[j * B_HEADS + h] for h in heads]
        for h in heads:
            if not carry:
                sout_ref[j, h] = state[h].T
            y_ref[rs, vss[h]] = _head_rms_gate(o[h], onorm_ref[...], gate_ref[rs, vss[h]])
    if carry:
        for h in heads:
            ss_ref[h] = state[h]

        @pl.when(_is_last_step())
        def _():
            for h in heads:
                sout_ref[0, h] = ss_ref[h].T


def _c_chunk_body(qkv_ref, z_ref, ba_ref, cw_ref, alog_ref, dtb_ref, onorm_ref, *rest, nchunks, carry):
    if carry:
        y_ref, cvout_ref, sout_ref, pv_ref, ss_ref = rest
    else:
        cvin_ref, sin_ref, y_ref, cvout_ref, sout_ref = rest

    def conv_act(x, pv):
        row8 = _iota((8, 1), 0)
        conv = x * cw_ref[C_CONV - 1:C_CONV, :]
        for j in range(1, C_CONV):
            sh = pltpu.roll(x, j, 0)
            head = jnp.where(row8 < j, pltpu.roll(pv, j, 0), sh[:8])
            sh = jnp.concatenate([head, sh[8:]], axis=0)
            conv = conv + sh * cw_ref[C_CONV - 1 - j:C_CONV - j, :]
        return _silu(conv)

    if carry:
        @pl.when(_is_first_step())
        def _():
            ss_ref[...] = jnp.zeros_like(ss_ref)
            pv_ref[...] = jnp.zeros_like(pv_ref)

        x = qkv_ref[...]
        act = conv_act(x, pv_ref[...])
        pv_ref[...] = x[nchunks * CHUNK - 8:]
    else:
        acts = []
        for j in range(nchunks):
            x = qkv_ref[_chunk_rows(j), :]
            acts.append(conv_act(x, cvin_ref[j]))
            cvout_ref[j] = x[CHUNK - 8:]
        act = jnp.concatenate(acts, axis=0)

    ba = ba_ref[...]
    beta_all = _sigmoid(ba)
    g_all = -jnp.exp(alog_ref[...]) * _softplus(ba + dtb_ref[...])
    tri = (_iota((CHUNK, CHUNK), 0) >= _iota((CHUNK, CHUNK), 1)).astype(BF16)
    triu2 = (_iota((CHUNK, 128), 0) <= _iota((CHUNK, 128), 1) % CHUNK).astype(BF16)
    ri = _iota((128, 128), 0)
    ci = _iota((128, 128), 1)
    same = (ri // CHUNK) == (ci // CHUNK)
    strict = same & (ri % CHUNK > ci % CHUNK)
    incl = same & (ri % CHUNK >= ci % CHUNK)
    lane_lo = _iota((1, 128), 1) < CHUNK
    zeros = jnp.zeros((CHUNK, C_DK), BF16)
    l2 = lambda t: t * lax.rsqrt(jnp.sum(t * t, axis=-1, keepdims=True) + 1e-6)
    blockdiag = lambda a, b: jnp.concatenate(
        [jnp.concatenate([a.astype(BF16), zeros], axis=1),
         jnp.concatenate([zeros, b.astype(BF16)], axis=1)], axis=0)

    heads = range(C_HEADS)
    npairs = C_HEADS // 2
    q, k, v, beta, kb, col, egc, glast = ({} for _ in range(8))
    gc_row = []
    for j in range(nchunks):
        rs = _chunk_rows(j)
        gc_col = _exact_left(tri, g_all[rs])
        gc_row.append(_exact_tn(g_all[rs], triu2))
        for h in heads:
            u = (j, h)
            q[u] = l2(act[rs, h * C_DK:(h + 1) * C_DK]) * (C_DK ** -0.5)
            k[u] = l2(act[rs, C_HEADS * C_DK + h * C_DK:C_HEADS * C_DK + (h + 1) * C_DK])
            v[u] = act[rs, 2 * C_HEADS * C_DK + h * C_DV:2 * C_HEADS * C_DK + (h + 1) * C_DV]
            beta[u] = beta_all[rs, h:h + 1]
            kb[u] = k[u] * beta[u]
            col[u] = gc_col[:, C_HEADS + h:C_HEADS + h + 1]
            egc[u] = jnp.exp(col[u])
            glast[u] = col[u][CHUNK - 1:CHUNK, :]
    units = [(j, p) for j in range(nchunks) for p in range(npairs)]
    decay, sc, rhs = [], [], []
    for j, p in units:
        a, b = (j, 2 * p), (j, 2 * p + 1)
        col_pair = jnp.concatenate([col[a], col[b]], axis=0)
        row_pair = jnp.where(lane_lo, gc_row[j][C_HEADS + 2 * p:C_HEADS + 2 * p + 1, :],
                             gc_row[j][C_HEADS + 2 * p + 1:C_HEADS + 2 * p + 2, :])
        decay.append(jnp.where(incl, jnp.exp(jnp.where(incl, col_pair - row_pair, 0.0)), 0.0))
        lhs = jnp.concatenate([blockdiag(kb[a], kb[b]), blockdiag(q[a], q[b])], axis=0)
        sc.append(_mm_nt(lhs, blockdiag(k[a], k[b])))
        rhs.append(jnp.concatenate(
            [jnp.concatenate([v[u] * beta[u], kb[u] * egc[u]], axis=1) for u in (a, b)], axis=0).astype(BF16))
    n_l = [-jnp.where(strict, s[:128], 0.0) * d for s, d in zip(sc, decay)]
    attn = [(s[128:] * d).astype(BF16) for s, d in zip(sc, decay)]
    tinv = _nilpotent_inverse(n_l)
    sol = [_mm(t, r) for t, r in zip(tinv, rhs)]

    if carry:
        state = [ss_ref[h] for h in heads]
    for j in range(nchunks):
        if not carry:
            state = [sin_ref[j, h] for h in heads]
        rs = _chunk_rows(j)
        half = lambda m, h: m[(h % 2) * CHUNK:(h % 2 + 1) * CHUNK]
        qw = [_mm(jnp.concatenate([q[j, h] * egc[j, h], half(sol[j * npairs + h // 2], h)[:, C_DV:]], axis=0),
                  state[h]) for h in heads]
        v_new = [(sol[j * npairs + p][:, :C_DV]
                  - jnp.concatenate([qw[2 * p][CHUNK:], qw[2 * p + 1][CHUNK:]], axis=0)).astype(BF16)
                 for p in range(npairs)]
        o_intra = [_mm(attn[j * npairs + p], v_new[p]) for p in range(npairs)]
        upd = [_mm_tn(k[j, h] * jnp.exp(glast[j, h] - col[j, h]), half(v_new[h // 2], h)) for h in heads]
        state = [state[h] * jnp.exp(glast[j, h]) + upd[h] for h in heads]
        for h in heads:
            o = qw[h][:CHUNK] + half(o_intra[h // 2], h)
            vs = slice(h * C_DV, (h + 1) * C_DV)
            y_ref[rs, vs] = _head_rms_gate(o, onorm_ref[...], z_ref[rs, vs])
            if not carry:
                sout_ref[j, h] = state[h]
    if carry:
        for h in heads:
            ss_ref[h] = state[h]

        @pl.when(_is_last_step())
        def _():
            sout_ref[0] = ss_ref[...]
            cvout_ref[0] = pv_ref[...]


def _pad_cols(w, n):
    return jnp.pad(w, ((0, 0), (0, n - w.shape[1])))


def kernel(x_prompt, x_sample, state_a_shift, state_a_wkv, state_b_kv, state_c_conv, state_c_kv, norm_mix, norm_ffn, norm_final, ffn_w_in, ffn_w_out, a_mu, a_w0, a_w1, a_w2, a_a0, a_a1, a_a2, a_g1, a_g2, a_k_k, a_k_a, a_r_k, a_w_rkv, a_w_o, a_lnx_w, a_lnx_b, b_w_in, b_w_a1, b_w_a2, b_b_a, b_onorm, b_w_o, c_w_in, c_conv_w, c_a_log, c_dt_bias, c_onorm, c_w_o):
    bp, seq, d = x_prompt.shape
    bd, dseq, _ = x_sample.shape
    depth = norm_mix.shape[0]
    cps, cpd = seq // CHUNK, dseq // CHUNK
    npch = bp * cps
    shape = (bp, seq, bd, dseq)
    x = jnp.concatenate([x_prompt.reshape(bp * seq, d), x_sample.reshape(bd * dseq, d)], axis=0)
    row = lambda a: a.reshape(1, -1)
    bf = lambda a: a.astype(BF16)

    last_prompt = [(s + 1) * cps - 1 for s in range(bp)]
    last_sample = [npch + (s + 1) * cpd - 1 for s in range(bd)]

    out_a_shift, out_a_wkv, out_b_kv, out_c_conv, out_c_kv = [], [], [], [], []
    for i in range(depth):
        j = i // 3
        gain = row(norm_mix[i])
        if i % 3 == 0:
            shift_rows = jnp.zeros((x.shape[0] // CHUNK, d), F32)
            shift_rows = shift_rows.at[jnp.asarray([npch + s * cpd for s in range(bd)])].set(state_a_shift[j])
            proj = _a_proj(
                x, shift_rows, gain, a_mu[j], row(a_w0[j]), row(a_a0[j]),
                bf(a_w_rkv[j, 0]), bf(a_w_rkv[j, 1]), bf(a_w_rkv[j, 2]),
                bf(a_w1[j]), bf(a_a1[j]), bf(a_g1[j]), bf(a_w2[j]), bf(a_a2[j]), bf(a_g2[j]), npch, cps, cpd)
            hl = proj[6]
            cols = [(a, D_MODEL, 0) for a in proj[:6]]
            shared = [row(a_k_k[j]), row(a_k_a[j]), row(a_r_k[j]), row(a_lnx_w[j]), row(a_lnx_b[j])]
            (y_p, wkv_p), (y_s, wkv_s) = _chunk_calls(
                _a_chunk_body, "rwkv_chunk", A_CHUNKS_PER_STEP, cols, cols, shared, [state_a_wkv[j]],
                [(A_HEADS, A_HEAD, A_HEAD)], D_MODEL, [pltpu.VMEM((A_PAIRS, 128, 128), F32)], *shape)
            out_a_shift.append((hl[jnp.asarray(last_prompt)], hl[jnp.asarray(last_sample)]))
            out_a_wkv.append((wkv_p, wkv_s))
            w_o = a_w_o[j]
        elif i % 3 == 1:
            w = jnp.concatenate([b_w_in[j], _pad_cols(b_w_a1[j], B_LORA_PAD)], axis=1)
            proj = _norm_proj(x, gain, bf(w))
            wa2 = jnp.pad(b_w_a2[j], ((0, B_LORA_PAD - b_w_a2.shape[1]), (0, 0)))
            qk_w, v_w = B_HEADS * B_DK, B_HEADS * B_DV
            cols = [(proj, qk_w, 0), (proj, qk_w, 1), (proj, v_w, 1), (proj, v_w, 2),
                    (proj, B_LORA_PAD, (2 * qk_w + 2 * v_w) // B_LORA_PAD)]
            shared = [bf(wa2), row(b_b_a[j]), row(b_onorm[j])]
            (y_p, kv_p), (y_s, kv_s) = _chunk_calls(
                _b_chunk_body, "gla_chunk", B_CHUNKS_PER_STEP, cols, cols, shared, [state_b_kv[j]],
                [(B_HEADS, B_DK, B_DV)], v_w, [pltpu.VMEM((B_HEADS, B_DV, B_DK), F32)], *shape)
            out_b_kv.append((kv_p, kv_s))
            w_o = b_w_o[j]
        else:
            proj = _norm_proj(x, gain, bf(_pad_cols(c_w_in[j], C_PROJ_PAD)))
            lanes = lambda a: jnp.pad(a.reshape(1, -1), ((0, 0), (C_HEADS, 128 - 2 * C_HEADS)))
            z_w = C_HEADS * C_DV
            cols = [(proj, C_QKV, 0), (proj, z_w, C_QKV // z_w), (proj, 128, (C_QKV + z_w) // 128)]
            shared = [c_conv_w[j], lanes(c_a_log[j]), lanes(c_dt_bias[j]), row(c_onorm[j])]
            cv_in = jnp.pad(state_c_conv[j], ((0, 0), (8 - (C_CONV - 1), 0), (0, 0)))
            (y_p, cv_p, kv_p), (y_s, cv_s, kv_s) = _chunk_calls(
                _c_chunk_body, "gdn_chunk", C_CHUNKS_PER_STEP, cols, cols, shared, [cv_in, state_c_kv[j]],
                [(8, C_QKV), (C_HEADS, C_DK, C_DV)], z_w,
                [pltpu.VMEM((8, C_QKV), F32), pltpu.VMEM((C_HEADS, C_DK, C_DV), F32)], *shape)
            out_c_conv.append((cv_p[:, 8 - (C_CONV - 1):], cv_s[:, 8 - (C_CONV - 1):]))
            out_c_kv.append((kv_p, kv_s))
            w_o = c_w_o[j]
        final = i == depth - 1
        res = _out_ffn(x, y_p, y_s, bf(w_o), row(norm_ffn[i]), bf(ffn_w_in[i]), bf(ffn_w_out[i]),
                       row(norm_final), final)
        if not final:
            x = res[0]

    split = lambda parts: (jnp.stack([p for p, _ in parts]), jnp.stack([s for _, s in parts]))
    a_shift_p, a_shift_s = split(out_a_shift)
    a_wkv_p, a_wkv_s = split(out_a_wkv)
    b_kv_p, b_kv_s = split(out_b_kv)
    c_conv_p, c_conv_s = split(out_c_conv)
    c_kv_p, c_kv_s = split(out_c_kv)
    y_prompt = res[0].reshape(bp, seq, d)
    y_sample = res[1].reshape(bd, dseq, d)
    return (y_prompt, y_sample, a_shift_p, a_wkv_p, b_kv_p, c_conv_p, c_kv_p,
            a_shift_s, a_wkv_s, b_kv_s, c_conv_s, c_kv_s)
```

```python
import functools
import math

import jax
import jax.numpy as jnp
from jax import lax
from jax.experimental import pallas as pl
from jax.experimental.pallas import tpu as pltpu

F32 = jnp.float32
BF16 = jnp.bfloat16

D_MODEL = 1024
CHUNK = 64
RMS_EPS = 1e-6
D_FF = 2816
FFN_COLS = 256
PROJ_COLS = 512

A_HEADS = 16
A_HEAD = 64
A_PAIRS = A_HEADS // 2
A_LNX_EPS = 64e-5
DECAY_SCALE = math.exp(-0.5)

B_HEADS = 4
B_DK = 128
B_DV = 256
B_GATE_TAU = 16.0
B_LORA_PAD = 128

C_HEADS = 8
C_DK = 128
C_DV = 128
C_CONV = 4
C_QKV = C_HEADS * (2 * C_DK + C_DV)
C_PROJ_PAD = C_QKV + C_HEADS * C_DV + 128

TOKEN_TILE = 512
A_CHUNKS_PER_STEP = 2
B_CHUNKS_PER_STEP = 8
C_CHUNKS_PER_STEP = 4
VMEM_LIMIT = 56 * 1024 * 1024


def _mm(a, b):
    return jnp.dot(a.astype(BF16), b.astype(BF16), preferred_element_type=F32)


def _mm_nt(a, b):
    return lax.dot_general(a.astype(BF16), b.astype(BF16), (((1,), (1,)), ((), ())),
                           preferred_element_type=F32)


def _mm_tn(a, b):
    return lax.dot_general(a.astype(BF16), b.astype(BF16), (((0,), (0,)), ((), ())),
                           preferred_element_type=F32)


def _split3(x):
    hi = x.astype(BF16)
    r1 = x - hi.astype(F32)
    mid = r1.astype(BF16)
    lo = (r1 - mid.astype(F32)).astype(BF16)
    return hi, mid, lo


def _exact_left(m_bf16, x):
    hi, mid, lo = _split3(x)
    dot = lambda p: jnp.dot(m_bf16, p, preferred_element_type=F32)
    return dot(hi) + dot(mid) + dot(lo)


def _exact_tn(x, m_bf16):
    hi, mid, lo = _split3(x)
    dot = lambda p: lax.dot_general(p, m_bf16, (((0,), (0,)), ((), ())), preferred_element_type=F32)
    return dot(hi) + dot(mid) + dot(lo)


def _rms(x, gain):
    return x * lax.rsqrt(jnp.mean(x * x, axis=-1, keepdims=True) + RMS_EPS) * gain


def _softplus(x):
    return jnp.maximum(x, 0.0) + jnp.log1p(jnp.exp(-jnp.abs(x)))


def _sigmoid(x):
    return 1.0 / (1.0 + jnp.exp(-x))


def _silu(x):
    return x * _sigmoid(x)


def _iota(shape, dim):
    return lax.broadcasted_iota(jnp.int32, shape, dim)


def _nilpotent_inverse(ns):
    eye = _iota((128, 128), 0) == _iota((128, 128), 1)
    plus_eye = lambda m: jnp.where(eye, 1.0, m).astype(BF16)
    inv = [plus_eye(n) for n in ns]
    nb = [n.astype(BF16) for n in ns]
    pw = [_mm(m, m) for m in nb]
    for step in range(5):
        rhs = [plus_eye(m) for m in pw]
        if step < 4:
            pwb = [m.astype(BF16) for m in pw]
            both = [_mm(jnp.concatenate([t, m], axis=0), r) for t, m, r in zip(inv, pwb, rhs)]
            inv = [b[:128].astype(BF16) for b in both]
            pw = [b[128:] - m.astype(F32) for b, m in zip(both, pwb)]
        else:
            inv = [_mm(t, r).astype(BF16) for t, r in zip(inv, rhs)]
    return inv


def _chunk_rows(j):
    return slice(j * CHUNK, (j + 1) * CHUNK)


def _is_first_step():
    return pl.program_id(1) == 0


def _is_last_step():
    return pl.program_id(1) == pl.num_programs(1) - 1


def _token_tiles(xp, xs):
    tm = TOKEN_TILE
    assert xp.shape[0] % tm == 0 and xs.shape[0] % tm == 0
    return xp.shape[0] // tm, xs.shape[0] // tm


def _split_specs(npt, rows, width):
    return [pl.BlockSpec((rows, width), lambda i: (jnp.minimum(i, npt - 1), 0)),
            pl.BlockSpec((rows, width), lambda i: (jnp.maximum(i - npt, 0), 0))]


def _split_shapes(xp, xs, width):
    return [jax.ShapeDtypeStruct((xp.shape[0], width), F32), jax.ShapeDtypeStruct((xs.shape[0], width), F32)]


def _select_tile(in_prompt, p_ref, s_ref):
    return jnp.where(in_prompt, p_ref[...], s_ref[...])


def _store_tile(in_prompt, p_ref, s_ref, value):
    @pl.when(in_prompt)
    def _():
        p_ref[...] = value

    @pl.when(jnp.logical_not(in_prompt))
    def _():
        s_ref[...] = value


def _proj_body(xp_ref, xs_ref, gain_ref, w_ref, o_ref, *, n, n_prompt_tiles):
    x = _select_tile(pl.program_id(0) < n_prompt_tiles, xp_ref, xs_ref)
    h = _rms(x, gain_ref[...]).astype(BF16)
    for lo in range(0, n, PROJ_COLS):
        cols = slice(lo, min(lo + PROJ_COLS, n))
        o_ref[:, cols] = jnp.dot(h, w_ref[:, cols], preferred_element_type=F32)


def _norm_proj(xp, xs, gain, w):
    n = w.shape[1]
    tm = TOKEN_TILE
    npt, nst = _token_tiles(xp, xs)
    return pl.pallas_call(
        functools.partial(_proj_body, n=n, n_prompt_tiles=npt),
        grid=(npt + nst,),
        in_specs=_split_specs(npt, tm, D_MODEL) + [pl.BlockSpec((1, D_MODEL), lambda i: (0, 0)),
                                                   pl.BlockSpec((D_MODEL, n), lambda i: (0, 0))],
        out_specs=pl.BlockSpec((tm, n), lambda i: (i, 0)),
        out_shape=jax.ShapeDtypeStruct(((npt + nst) * tm, n), F32),
        compiler_params=pltpu.CompilerParams(
            dimension_semantics=("arbitrary",), vmem_limit_bytes=VMEM_LIMIT),
        name="norm_proj",
    )(xp, xs, gain, w)


def _ffn_body(xp_ref, xs_ref, yp_ref, ys_ref, wo_ref, gain_ref, win_ref, wout_ref, fgain_ref,
              op_ref, os_ref, acc_ref, *, n_prompt_tiles, final):
    in_prompt = pl.program_id(0) < n_prompt_tiles
    x = _select_tile(in_prompt, xp_ref, xs_ref)
    y = _select_tile(in_prompt, yp_ref, ys_ref)
    x1 = x + jnp.dot(y.astype(BF16), wo_ref[...], preferred_element_type=F32)
    h = _rms(x1, gain_ref[...]).astype(BF16)
    acc_ref[...] = x1
    for c in range(D_FF // FFN_COLS):
        lo = c * FFN_COLS
        g = jnp.dot(h, win_ref[:, lo:lo + FFN_COLS], preferred_element_type=F32)
        u = jnp.dot(h, win_ref[:, D_FF + lo:D_FF + lo + FFN_COLS], preferred_element_type=F32)
        a = (_silu(g) * u).astype(BF16)
        acc_ref[...] += jnp.dot(a, wout_ref[lo:lo + FFN_COLS, :], preferred_element_type=F32)
    out = acc_ref[...]
    _store_tile(in_prompt, op_ref, os_ref, _rms(out, fgain_ref[...]) if final else out)


def _out_ffn(xp, xs, y_prompt, y_sample, wo, gain, win, wout, layer, fgain, final):
    tm = TOKEN_TILE
    npt, nst = _token_tiles(xp, xs)
    fixed = lambda i: (0, 0)
    of_layer = lambda i: (layer, 0, 0)
    return pl.pallas_call(
        functools.partial(_ffn_body, n_prompt_tiles=npt, final=final),
        grid=(npt + nst,),
        in_specs=_split_specs(npt, tm, D_MODEL) + _split_specs(npt, tm, D_MODEL) + [
            pl.BlockSpec((D_MODEL, D_MODEL), fixed),
            pl.BlockSpec((1, D_MODEL), fixed),
            pl.BlockSpec((None, D_MODEL, 2 * D_FF), of_layer),
            pl.BlockSpec((None, D_FF, D_MODEL), of_layer),
            pl.BlockSpec((1, D_MODEL), fixed)],
        out_specs=_split_specs(npt, tm, D_MODEL),
        out_shape=_split_shapes(xp, xs, D_MODEL),
        scratch_shapes=[pltpu.VMEM((tm, D_MODEL), F32)],
        compiler_params=pltpu.CompilerParams(
            dimension_semantics=("arbitrary",), vmem_limit_bytes=VMEM_LIMIT),
        name="out_ffn",
    )(xp, xs, y_prompt, y_sample, wo, gain, win, wout, fgain)


def _a_proj_body(xp_ref, xs_ref, bp_ref, bs_ref, sh_ref, gain_ref, mu_ref, w0_ref, a0_ref,
                 wr_ref, wk_ref, wv_ref, w1_ref, a1_ref, g1_ref, w2_ref, a2_ref, g2_ref,
                 r_ref, k_ref, v_ref, lw_ref, al_ref, gg_ref, hl_ref, *, tm, n_prompt_tiles, npch, cps, cpd):
    i = pl.program_id(0)
    in_prompt = i < n_prompt_tiles
    gain = gain_ref[...]
    h = _rms(_select_tile(in_prompt, xp_ref, xs_ref), gain)
    prev_tile_last = _rms(_select_tile(in_prompt, bp_ref, bs_ref), gain)[7:8, :]
    row = _iota((CHUNK, 1), 0)
    pieces = []
    for c in range(tm // CHUNK):
        hc = h[_chunk_rows(c)]
        natural = prev_tile_last if c == 0 else h[c * CHUNK - 1:c * CHUNK]
        gc = i * (tm // CHUNK) + c
        start = jnp.where(gc < npch, gc % cps == 0, (gc - npch) % cpd == 0)
        first = jnp.where(start, sh_ref[c:c + 1, :], natural)
        pieces.append(jnp.where(row == 0, first, pltpu.roll(hc, 1, 0)))
        hl_ref[c:c + 1, :] = hc[CHUNK - 1:CHUNK]
    xx = jnp.concatenate(pieces, axis=0) - h
    mix = lambda j: (h + xx * mu_ref[j:j + 1, :]).astype(BF16)
    dot = lambda a, w_ref: jnp.dot(a, w_ref[...], preferred_element_type=F32)
    r_ref[...] = dot(mix(0), wr_ref)
    z = w0_ref[...] + _mm(jnp.tanh(dot(mix(1), w1_ref)), w2_ref[...])
    lw_ref[...] = (-DECAY_SCALE) * _sigmoid(z)
    k_ref[...] = dot(mix(2), wk_ref)
    v_ref[...] = dot(mix(3), wv_ref)
    al_ref[...] = _sigmoid(a0_ref[...] + _mm(dot(mix(4), a1_ref), a2_ref[...]))
    gg_ref[...] = _mm(_sigmoid(dot(mix(5), g1_ref)), g2_ref[...])


def _a_proj(xp, xs, shift_rows, gain, mu, w0, a0, wr, wk, wv, w1, a1, g1, w2, a2, g2, npch, cps, cpd):
    tm = TOKEN_TILE
    npt, nst = _token_tiles(xp, xs)
    nt = (npt + nst) * tm
    row = lambda i: (i, 0)
    fixed = lambda i: (0, 0)
    full = lambda a: pl.BlockSpec(a.shape, fixed)
    tok = pl.BlockSpec((tm, D_MODEL), row)
    tok_shape = jax.ShapeDtypeStruct((nt, D_MODEL), F32)
    per8 = tm // 8
    before = [pl.BlockSpec((8, D_MODEL), lambda i: (jnp.clip(i * per8 - 1, 0, npt * per8 - 1), 0)),
              pl.BlockSpec((8, D_MODEL), lambda i: (jnp.clip((i - npt) * per8 - 1, 0, nst * per8 - 1), 0))]
    return pl.pallas_call(
        functools.partial(_a_proj_body, tm=tm, n_prompt_tiles=npt, npch=npch, cps=cps, cpd=cpd),
        grid=(npt + nst,),
        in_specs=_split_specs(npt, tm, D_MODEL) + before + [
            pl.BlockSpec((tm // CHUNK, D_MODEL), row),
            full(gain), full(mu), full(w0), full(a0), full(wr), full(wk), full(wv),
            full(w1), full(a1), full(g1), full(w2), full(a2), full(g2)],
        out_specs=[tok] * 6 + [pl.BlockSpec((tm // CHUNK, D_MODEL), row)],
        out_shape=[tok_shape] * 6 + [jax.ShapeDtypeStruct((nt // CHUNK, D_MODEL), F32)],
        compiler_params=pltpu.CompilerParams(
            dimension_semantics=("arbitrary",), vmem_limit_bytes=VMEM_LIMIT),
        name="rwkv_proj",
    )(xp, xs, xp, xs, shift_rows, gain, mu, w0, a0, wr, wk, wv, w1, a1, g1, w2, a2, g2)


def _a_chunk_body(r_ref, k_ref, v_ref, lw_ref, al_ref, gg_ref, kk_ref, ka_ref, rk_ref, lnw_ref, lnb_ref,
                  *rest, nchunks, carry):
    if carry:
        y_ref, sout_ref, ss_ref = rest
    else:
        sin_ref, y_ref, sout_ref = rest

    lane = _iota((1, 128), 1)
    head0 = lane < A_HEAD
    ri = _iota((128, 128), 0) % CHUNK
    ci = _iota((128, 128), 1) % CHUNK
    strict = ri > ci
    incl = ri >= ci
    tri = (_iota((CHUNK, CHUNK), 0) >= _iota((CHUNK, CHUNK), 1)).astype(BF16)
    zero = jnp.zeros((A_HEAD, A_HEAD), F32)
    same_head = (_iota((128, 128), 0) // A_HEAD) == (_iota((128, 128), 1) // A_HEAD)
    head_mean = jnp.where(same_head, 1.0 / A_HEAD, 0.0).astype(BF16)

    def headsum(x):
        s0 = jnp.sum(jnp.where(head0, x, 0.0), axis=-1, keepdims=True)
        s1 = jnp.sum(jnp.where(head0, 0.0, x), axis=-1, keepdims=True)
        return jnp.where(head0, s0, s1)

    def expand(x):
        xb = x.astype(BF16)
        zb = jnp.zeros_like(xb)
        return jnp.concatenate([jnp.where(head0, xb, zb), jnp.where(head0, zb, xb)], axis=0)

    def load_state(ref, j):
        return [jnp.concatenate([jnp.concatenate([ref[j, 2 * p], zero], axis=1),
                                 jnp.concatenate([zero, ref[j, 2 * p + 1]], axis=1)], axis=0)
                for p in range(A_PAIRS)]

    def store_state(ref, j, state):
        for p in range(A_PAIRS):
            ref[j, 2 * p] = state[p][:A_HEAD, :A_HEAD]
            ref[j, 2 * p + 1] = state[p][A_HEAD:, A_HEAD:]

    pairs = range(A_PAIRS)
    sls = [slice(p * 128, (p + 1) * 128) for p in pairs]
    units = [(j, p) for j in range(nchunks) for p in pairs]
    gcum_all = [_exact_left(tri, lw_ref[_chunk_rows(j), :]) for j in range(nchunks)]
    xs, ys, ves, ends, glasts, bonus_sums = [], [], [], [], [], []
    for j, p in units:
        rs, sl = _chunk_rows(j), sls[p]
        k = k_ref[rs, sl]
        lw = lw_ref[rs, sl]
        al = al_ref[rs, sl]
        kk = k * kk_ref[:, sl]
        kk = kk * lax.rsqrt(jnp.maximum(headsum(kk * kk), 1e-24))
        kmod = k * (1.0 + (al - 1.0) * ka_ref[:, sl])
        gcum = gcum_all[j][:, sl]
        glast = gcum[CHUNK - 1:CHUNK, :]
        dec_out = jnp.exp(-gcum)
        dec_end = jnp.exp(glast - gcum)
        ae = expand(-kk * jnp.exp(gcum - lw))
        re = expand(r_ref[rs, sl] * jnp.exp(gcum))
        be = expand(kk * al * dec_out)
        ke = expand(kmod * dec_out)
        xs.append(jnp.concatenate([ae, re], axis=0))
        ys.append(jnp.concatenate([be, ke], axis=0))
        ves.append(expand(v_ref[rs, sl]))
        ends.append(jnp.concatenate([expand(kk * al * dec_end), expand(kmod * dec_end)], axis=0))
        glasts.append(glast)
        bonus_sums.append(headsum(r_ref[rs, sl] * kmod * rk_ref[:, sl]))

    sc = [_mm_nt(x, y) for x, y in zip(xs, ys)]
    n_ab = [jnp.where(strict, s[:128, :128], 0.0) for s in sc]
    l_ak = [jnp.where(strict, s[:128, 128:], 0.0).astype(BF16) for s in sc]
    m_rb = [jnp.where(incl, s[128:, :128], 0.0).astype(BF16) for s in sc]
    m_rk = [jnp.where(incl, s[128:, 128:], 0.0).astype(BF16) for s in sc]
    tinv = _nilpotent_inverse(n_ab)
    lv = [_mm(l, ve).astype(BF16) for l, ve in zip(l_ak, ves)]
    pq = [_mm(t, jnp.concatenate([x[:128], l], axis=1)).astype(BF16)
          for t, x, l in zip(tinv, xs, lv)]
    mt = [_mm(m, z) for m, z in zip(m_rb, pq)]
    mv = [_mm(m, ve) for m, ve in zip(m_rk, ves)]

    if carry:
        @pl.when(_is_first_step())
        def _():
            ss_ref[...] = jnp.zeros_like(ss_ref)

        state = [ss_ref[p] for p in pairs]
    for j in range(nchunks):
        if not carry:
            state = load_state(sin_ref, j)
        un = [j * A_PAIRS + p for p in pairs]
        lhs = [jnp.concatenate([(xs[u][128:] + mt[u][:, :128]).astype(BF16), pq[u][:, :128]], axis=0) for u in un]
        ou = [_mm_nt(a, s) for a, s in zip(lhs, state)]
        oe = [o[:128] + mt[u][:, 128:] + mv[u] for o, u in zip(ou, un)]
        uv = [jnp.concatenate([(o[128:] + pq[u][:, 128:]).astype(BF16), ves[u]], axis=0) for o, u in zip(ou, un)]
        upd = [_mm_tn(a, ends[u]) for a, u in zip(uv, un)]
        state = [s * jnp.exp(glasts[u]) + d for s, u, d in zip(state, un, upd)]
        if not carry:
            store_state(sout_ref, j, state)
        rs = _chunk_rows(j)
        os = [o[:CHUNK] + o[CHUNK:] for o in oe]
        parts = []
        for o in os:
            m = jnp.concatenate([o, o * o], axis=0)
            hi = m.astype(BF16)
            parts.append(jnp.concatenate([hi, (m - hi.astype(F32)).astype(BF16)], axis=0))
        mom = [jnp.dot(m, head_mean, preferred_element_type=F32) for m in parts]
        for p, sl in enumerate(sls):
            mean = mom[p][:CHUNK] + mom[p][2 * CHUNK:3 * CHUNK]
            var = mom[p][CHUNK:2 * CHUNK] + mom[p][3 * CHUNK:] - mean * mean
            on = (os[p] - mean) * lax.rsqrt(var + A_LNX_EPS) * lnw_ref[:, sl] + lnb_ref[:, sl]
            y_ref[rs, sl] = (on + bonus_sums[un[p]] * v_ref[rs, sl]) * gg_ref[rs, sl]
    if carry:
        for p in pairs:
            ss_ref[p] = state[p]

        @pl.when(_is_last_step())
        def _():
            store_state(sout_ref, 0, [ss_ref[p] for p in pairs])


def _chunk_calls(body, name, k, prompt_in, sample_in, shared, state_in, state_shapes, y_width, scratch,
                 bp, seq, bd, dseq, layer=None):
    assert dseq == CHUNK, "sample streams are processed as single chunks"
    rows = k * CHUNK
    steps = seq // rows
    params = pltpu.CompilerParams(dimension_semantics=("arbitrary",) * 2, vmem_limit_bytes=VMEM_LIMIT)

    def run(carry, grid, tok_index, col_specs, states, out_rows, nstate, state_index):
        ndim = len(grid)
        fixed = lambda a: pl.BlockSpec(a.shape, lambda *g: (0,) * a.ndim)
        tok_specs = [pl.BlockSpec((rows, w), functools.partial(lambda c, *g: (tok_index(*g), c), c))
                     for (_, w, c) in col_specs]
        st_specs = [pl.BlockSpec((nstate,) + s, functools.partial(lambda n, *g: (state_index(*g),) + (0,) * n, len(s)))
                    for s in state_shapes]
        st_in_specs = st_specs if layer is None else [
            pl.BlockSpec((None, nstate) + s,
                         functools.partial(lambda n, *g: (layer, state_index(*g)) + (0,) * n, len(s)))
            for s in state_shapes]
        y_spec = pl.BlockSpec((rows, y_width), lambda *g: (tok_index(*g) - tok_index(*(0,) * ndim), 0))
        return pl.pallas_call(
            functools.partial(body, nchunks=k, carry=carry),
            grid=grid,
            in_specs=tok_specs + [fixed(a) for a in shared] + (st_in_specs if states else []),
            out_specs=[y_spec] + st_specs,
            out_shape=[jax.ShapeDtypeStruct((out_rows, y_width), F32)]
            + [jax.ShapeDtypeStruct((grid[0] * nstate if not carry else grid[0],) + s, F32) for s in state_shapes],
            scratch_shapes=scratch if carry else [],
            compiler_params=params if ndim == 2 else pltpu.CompilerParams(
                dimension_semantics=("arbitrary",), vmem_limit_bytes=VMEM_LIMIT),
            name=name + ("_prompt" if carry else "_sample"),
        )(*[a for (a, _, _) in col_specs], *shared, *states)

    prompt = run(True, (bp, steps), lambda b, i: b * steps + i, prompt_in, [], bp * seq, 1, lambda b, i: b)
    off = bp * seq // rows
    sample = run(False, (bd // k,), lambda i: off + i, sample_in, state_in, bd * dseq, k, lambda i: i)
    return prompt, sample


def _head_rms_gate(o, gain, gate):
    return o * lax.rsqrt(jnp.mean(o * o, axis=-1, keepdims=True) + RMS_EPS) * gain * _silu(gate)


def _b_chunk_body(q_ref, k_ref, v_ref, gate_ref, lo_ref, wa2_ref, ba_ref, onorm_ref, *rest, nchunks, carry):
    if carry:
        y_ref, sout_ref, ss_ref = rest
    else:
        sin_ref, y_ref, sout_ref = rest
    incl = _iota((CHUNK, CHUNK), 0) >= _iota((CHUNK, CHUNK), 1)
    tri = incl.astype(BF16)
    heads = range(B_HEADS)
    kss = [slice(h * B_DK, (h + 1) * B_DK) for h in heads]
    vss = [slice(h * B_DV, (h + 1) * B_DV) for h in heads]
    units = [(j, h) for j in range(nchunks) for h in heads]
    gk_all = -_softplus(-(_mm(lo_ref[...], wa2_ref[...]) + ba_ref[...])) * (1.0 / B_GATE_TAU)
    b_all = [_exact_left(tri, gk_all[_chunk_rows(j)]) for j in range(nchunks)]
    qt, kt, khat, blast = [], [], [], []
    for j, h in units:
        rs, ks = _chunk_rows(j), kss[h]
        b = b_all[j][:, ks]
        k = k_ref[rs, ks]
        blast.append(b[CHUNK - 1:CHUNK, :])
        qt.append(q_ref[rs, ks] * (B_DK ** -0.5) * jnp.exp(b))
        kt.append(k * jnp.exp(-b))
        khat.append(k * jnp.exp(blast[-1] - b))
    a = [jnp.where(incl, _mm_nt(x, y), 0.0) for x, y in zip(qt, kt)]
    o_intra = [_mm(a[u], v_ref[_chunk_rows(j), vss[h]]) for u, (j, h) in enumerate(units)]
    upd = [_mm_tn(v_ref[_chunk_rows(j), vss[h]], khat[u]) for u, (j, h) in enumerate(units)]

    if carry:
        @pl.when(_is_first_step())
        def _():
            ss_ref[...] = jnp.zeros_like(ss_ref)

        state = [ss_ref[h] for h in heads]
    for j in range(nchunks):
        if not carry:
            state = [sin_ref[j, h].T for h in heads]
        rs = _chunk_rows(j)
        o = [o_intra[j * B_HEADS + h] + _mm_nt(qt[j * B_HEADS + h], state[h]) for h in heads]
        state = [state[h] * jnp.exp(blast[j * B_HEADS + h]) + upd[j * B_HEADS + h] for h in heads]
        for h in heads:
            if not carry:
                sout_ref[j, h] = state[h].T
            y_ref[rs, vss[h]] = _head_rms_gate(o[h], onorm_ref[...], gate_ref[rs, vss[h]])
    if carry:
        for h in heads:
            ss_ref[h] = state[h]

        @pl.when(_is_last_step())
        def _():
            for h in heads:
                sout_ref[0, h] = ss_ref[h].T


def _c_chunk_body(qkv_ref, z_ref, ba_ref, cw_ref, alog_ref, dtb_ref, onorm_ref, *rest, nchunks, carry):
    if carry:
        y_ref, cvout_ref, sout_ref, pv_ref, ss_ref = rest
    else:
        cvin_ref, sin_ref, y_ref, cvout_ref, sout_ref = rest

    def conv_act(x, pv):
        row8 = _iota((8, 1), 0)
        conv = x * cw_ref[C_CONV - 1:C_CONV, :]
        for j in range(1, C_CONV):
            sh = pltpu.roll(x, j, 0)
            head = jnp.where(row8 < j, pltpu.roll(pv, j, 0), sh[:8])
            sh = jnp.concatenate([head, sh[8:]], axis=0)
            conv = conv + sh * cw_ref[C_CONV - 1 - j:C_CONV - j, :]
        return _silu(conv)

    if carry:
        @pl.when(_is_first_step())
        def _():
            ss_ref[...] = jnp.zeros_like(ss_ref)
            pv_ref[...] = jnp.zeros_like(pv_ref)

        x = qkv_ref[...]
        act = conv_act(x, pv_ref[...])
        pv_ref[...] = x[nchunks * CHUNK - 8:]
    else:
        acts = []
        for j in range(nchunks):
            x = qkv_ref[_chunk_rows(j), :]
            acts.append(conv_act(x, cvin_ref[j]))
            cvout_ref[j] = x[CHUNK - 8:]
        act = jnp.concatenate(acts, axis=0)

    ba = ba_ref[...]
    beta_all = _sigmoid(ba)
    g_all = -jnp.exp(alog_ref[...]) * _softplus(ba + dtb_ref[...])
    tri = (_iota((CHUNK, CHUNK), 0) >= _iota((CHUNK, CHUNK), 1)).astype(BF16)
    triu2 = (_iota((CHUNK, 128), 0) <= _iota((CHUNK, 128), 1) % CHUNK).astype(BF16)
    ri = _iota((128, 128), 0)
    ci = _iota((128, 128), 1)
    same = (ri // CHUNK) == (ci // CHUNK)
    strict = same & (ri % CHUNK > ci % CHUNK)
    incl = same & (ri % CHUNK >= ci % CHUNK)
    lane_lo = _iota((1, 128), 1) < CHUNK
    zeros = jnp.zeros((CHUNK, C_DK), BF16)
    l2 = lambda t: t * lax.rsqrt(jnp.sum(t * t, axis=-1, keepdims=True) + 1e-6)
    blockdiag = lambda a, b: jnp.concatenate(
        [jnp.concatenate([a.astype(BF16), zeros], axis=1),
         jnp.concatenate([zeros, b.astype(BF16)], axis=1)], axis=0)

    heads = range(C_HEADS)
    npairs = C_HEADS // 2
    q, k, v, beta, kb, col, egc, glast = ({} for _ in range(8))
    gc_row = []
    for j in range(nchunks):
        rs = _chunk_rows(j)
        gc_col = _exact_left(tri, g_all[rs])
        gc_row.append(_exact_tn(g_all[rs], triu2))
        for h in heads:
            u = (j, h)
            q[u] = l2(act[rs, h * C_DK:(h + 1) * C_DK]) * (C_DK ** -0.5)
            k[u] = l2(act[rs, C_HEADS * C_DK + h * C_DK:C_HEADS * C_DK + (h + 1) * C_DK])
            v[u] = act[rs, 2 * C_HEADS * C_DK + h * C_DV:2 * C_HEADS * C_DK + (h + 1) * C_DV]
            beta[u] = beta_all[rs, h:h + 1]
            kb[u] = k[u] * beta[u]
            col[u] = gc_col[:, C_HEADS + h:C_HEADS + h + 1]
            egc[u] = jnp.exp(col[u])
            glast[u] = col[u][CHUNK - 1:CHUNK, :]
    units = [(j, p) for j in range(nchunks) for p in range(npairs)]
    decay, sc, rhs = [], [], []
    for j, p in units:
        a, b = (j, 2 * p), (j, 2 * p + 1)
        col_pair = jnp.concatenate([col[a], col[b]], axis=0)
        row_pair = jnp.where(lane_lo, gc_row[j][C_HEADS + 2 * p:C_HEADS + 2 * p + 1, :],
                             gc_row[j][C_HEADS + 2 * p + 1:C_HEADS + 2 * p + 2, :])
        decay.append(jnp.where(incl, jnp.exp(jnp.where(incl, col_pair - row_pair, 0.0)), 0.0))
        lhs = jnp.concatenate([blockdiag(kb[a], kb[b]), blockdiag(q[a], q[b])], axis=0)
        sc.append(_mm_nt(lhs, blockdiag(k[a], k[b])))
        rhs.append(jnp.concatenate(
            [jnp.concatenate([v[u] * beta[u], kb[u] * egc[u]], axis=1) for u in (a, b)], axis=0).astype(BF16))
    n_l = [-jnp.where(strict, s[:128], 0.0) * d for s, d in zip(sc, decay)]
    attn = [(s[128:] * d).astype(BF16) for s, d in zip(sc, decay)]
    tinv = _nilpotent_inverse(n_l)
    sol = [_mm(t, r) for t, r in zip(tinv, rhs)]

    if carry:
        state = [ss_ref[h] for h in heads]
    for j in range(nchunks):
        if not carry:
            state = [sin_ref[j, h] for h in heads]
        rs = _chunk_rows(j)
        half = lambda m, h: m[(h % 2) * CHUNK:(h % 2 + 1) * CHUNK]
        qw = [_mm(jnp.concatenate([q[j, h] * egc[j, h], half(sol[j * npairs + h // 2], h)[:, C_DV:]], axis=0),
                  state[h]) for h in heads]
        v_new = [(sol[j * npairs + p][:, :C_DV]
                  - jnp.concatenate([qw[2 * p][CHUNK:], qw[2 * p + 1][CHUNK:]], axis=0)).astype(BF16)
                 for p in range(npairs)]
        o_intra = [_mm(attn[j * npairs + p], v_new[p]) for p in range(npairs)]
        upd = [_mm_tn(k[j, h] * jnp.exp(glast[j, h] - col[j, h]), half(v_new[h // 2], h)) for h in heads]
        state = [state[h] * jnp.exp(glast[j, h]) + upd[h] for h in heads]
        for h in heads:
            o = qw[h][:CHUNK] + half(o_intra[h // 2], h)
            vs = slice(h * C_DV, (h + 1) * C_DV)
            y_ref[rs, vs] = _head_rms_gate(o, onorm_ref[...], z_ref[rs, vs])
            if not carry:
                sout_ref[j, h] = state[h]
    if carry:
        for h in heads:
            ss_ref[h] = state[h]

        @pl.when(_is_last_step())
        def _():
            sout_ref[0] = ss_ref[...]
            cvout_ref[0] = pv_ref[...]


def _pad_cols(w, n):
    return jnp.pad(w, ((0, 0), (0, n - w.shape[1])))


def kernel(x_prompt, x_sample, state_a_shift, state_a_wkv, state_b_kv, state_c_conv, state_c_kv, norm_mix, norm_ffn, norm_final, ffn_w_in, ffn_w_out, a_mu, a_w0, a_w1, a_w2, a_a0, a_a1, a_a2, a_g1, a_g2, a_k_k, a_k_a, a_r_k, a_w_rkv, a_w_o, a_lnx_w, a_lnx_b, b_w_in, b_w_a1, b_w_a2, b_b_a, b_onorm, b_w_o, c_w_in, c_conv_w, c_a_log, c_dt_bias, c_onorm, c_w_o):
    bp, seq, d = x_prompt.shape
    bd, dseq, _ = x_sample.shape
    depth = norm_mix.shape[0]
    cps, cpd = seq // CHUNK, dseq // CHUNK
    npch = bp * cps
    shape = (bp, seq, bd, dseq)
    xp, xs = x_prompt.reshape(bp * seq, d), x_sample.reshape(bd * dseq, d)
    n_chunks = (bp * seq + bd * dseq) // CHUNK
    row = lambda a: a.reshape(1, -1)
    bf = lambda a: a.astype(BF16)
    ffn_in, ffn_out = bf(ffn_w_in), bf(ffn_w_out)

    last_prompt = [(s + 1) * cps - 1 for s in range(bp)]
    last_sample = [npch + (s + 1) * cpd - 1 for s in range(bd)]

    out_a_shift, out_a_wkv, out_b_kv, out_c_conv, out_c_kv = [], [], [], [], []
    for i in range(depth):
        j = i // 3
        gain = row(norm_mix[i])
        if i % 3 == 0:
            shift_rows = jnp.zeros((n_chunks, d), F32)
            shift_rows = shift_rows.at[jnp.asarray([npch + s * cpd for s in range(bd)])].set(state_a_shift[j])
            proj = _a_proj(
                xp, xs, shift_rows, gain, a_mu[j], row(a_w0[j]), row(a_a0[j]),
                bf(a_w_rkv[j, 0]), bf(a_w_rkv[j, 1]), bf(a_w_rkv[j, 2]),
                bf(a_w1[j]), bf(a_a1[j]), bf(a_g1[j]), bf(a_w2[j]), bf(a_a2[j]), bf(a_g2[j]), npch, cps, cpd)
            hl = proj[6]
            cols = [(a, D_MODEL, 0) for a in proj[:6]]
            shared = [row(a_k_k[j]), row(a_k_a[j]), row(a_r_k[j]), row(a_lnx_w[j]), row(a_lnx_b[j])]
            (y_p, wkv_p), (y_s, wkv_s) = _chunk_calls(
                _a_chunk_body, "rwkv_chunk", A_CHUNKS_PER_STEP, cols, cols, shared, [state_a_wkv],
                [(A_HEADS, A_HEAD, A_HEAD)], D_MODEL, [pltpu.VMEM((A_PAIRS, 128, 128), F32)], *shape, layer=j)
            out_a_shift.append((hl[jnp.asarray(last_prompt)], hl[jnp.asarray(last_sample)]))
            out_a_wkv.append((wkv_p, wkv_s))
            w_o = a_w_o[j]
        elif i % 3 == 1:
            w = jnp.concatenate([b_w_in[j], _pad_cols(b_w_a1[j], B_LORA_PAD)], axis=1)
            proj = _norm_proj(xp, xs, gain, bf(w))
            wa2 = jnp.pad(b_w_a2[j], ((0, B_LORA_PAD - b_w_a2.shape[1]), (0, 0)))
            qk_w, v_w = B_HEADS * B_DK, B_HEADS * B_DV
            cols = [(proj, qk_w, 0), (proj, qk_w, 1), (proj, v_w, 1), (proj, v_w, 2),
                    (proj, B_LORA_PAD, (2 * qk_w + 2 * v_w) // B_LORA_PAD)]
            shared = [bf(wa2), row(b_b_a[j]), row(b_onorm[j])]
            (y_p, kv_p), (y_s, kv_s) = _chunk_calls(
                _b_chunk_body, "gla_chunk", B_CHUNKS_PER_STEP, cols, cols, shared, [state_b_kv[j]],
                [(B_HEADS, B_DK, B_DV)], v_w, [pltpu.VMEM((B_HEADS, B_DV, B_DK), F32)], *shape)
            out_b_kv.append((kv_p, kv_s))
            w_o = b_w_o[j]
        else:
            proj = _norm_proj(xp, xs, gain, bf(_pad_cols(c_w_in[j], C_PROJ_PAD)))
            lanes = lambda a: jnp.pad(a.reshape(1, -1), ((0, 0), (C_HEADS, 128 - 2 * C_HEADS)))
            z_w = C_HEADS * C_DV
            cols = [(proj, C_QKV, 0), (proj, z_w, C_QKV // z_w), (proj, 128, (C_QKV + z_w) // 128)]
            shared = [c_conv_w[j], lanes(c_a_log[j]), lanes(c_dt_bias[j]), row(c_onorm[j])]
            cv_in = jnp.pad(state_c_conv[j], ((0, 0), (8 - (C_CONV - 1), 0), (0, 0)))
            (y_p, cv_p, kv_p), (y_s, cv_s, kv_s) = _chunk_calls(
                _c_chunk_body, "gdn_chunk", C_CHUNKS_PER_STEP, cols, cols, shared, [cv_in, state_c_kv[j]],
                [(8, C_QKV), (C_HEADS, C_DK, C_DV)], z_w,
                [pltpu.VMEM((8, C_QKV), F32), pltpu.VMEM((C_HEADS, C_DK, C_DV), F32)], *shape)
            out_c_conv.append((cv_p[:, 8 - (C_CONV - 1):], cv_s[:, 8 - (C_CONV - 1):]))
            out_c_kv.append((kv_p, kv_s))
            w_o = c_w_o[j]
        final = i == depth - 1
        xp, xs = _out_ffn(xp, xs, y_p, y_s, bf(w_o), row(norm_ffn[i]), ffn_in, ffn_out, i,
                          row(norm_final), final)

    split = lambda parts: (jnp.stack([p for p, _ in parts]), jnp.stack([s for _, s in parts]))
    a_shift_p, a_shift_s = split(out_a_shift)
    a_wkv_p, a_wkv_s = split(out_a_wkv)
    b_kv_p, b_kv_s = split(out_b_kv)
    c_conv_p, c_conv_s = split(out_c_conv)
    c_kv_p, c_kv_s = split(out_c_kv)
    y_prompt = xp.reshape(bp, seq, d)
    y_sample = xs.reshape(bd, dseq, d)
    return (y_prompt, y_sample, a_shift_p, a_wkv_p, b_kv_p, c_conv_p, c_kv_p,
            a_shift_s, a_wkv_s, b_kv_s, c_conv_s, c_kv_s)
```

```python
import functools
import math

import jax
import jax.numpy as jnp
from jax import lax
from jax.experimental import pallas as pl
from jax.experimental.pallas import tpu as pltpu

F32 = jnp.float32
BF16 = jnp.bfloat16

D_MODEL = 1024
CHUNK = 64
RMS_EPS = 1e-6
D_FF = 2816
FFN_COLS = 256
PROJ_COLS = 512

A_HEADS = 16
A_HEAD = 64
A_PAIRS = A_HEADS // 2
A_LNX_EPS = 64e-5
DECAY_SCALE = math.exp(-0.5)

B_HEADS = 4
B_DK = 128
B_DV = 256
B_GATE_TAU = 16.0
B_LORA_PAD = 128

C_HEADS = 8
C_DK = 128
C_DV = 128
C_CONV = 4
C_QKV = C_HEADS * (2 * C_DK + C_DV)
C_PROJ_PAD = C_QKV + C_HEADS * C_DV + 128

TOKEN_TILE = 512
A_CHUNKS_PER_STEP = 2
B_CHUNKS_PER_STEP = 8
C_CHUNKS_PER_STEP = 4
VMEM_LIMIT = 56 * 1024 * 1024


def _mm(a, b):
    return jnp.dot(a.astype(BF16), b.astype(BF16), preferred_element_type=F32)


def _mm_nt(a, b):
    return lax.dot_general(a.astype(BF16), b.astype(BF16), (((1,), (1,)), ((), ())),
                           preferred_element_type=F32)


def _mm_tn(a, b):
    return lax.dot_general(a.astype(BF16), b.astype(BF16), (((0,), (0,)), ((), ())),
                           preferred_element_type=F32)


def _split3(x):
    hi = x.astype(BF16)
    r1 = x - hi.astype(F32)
    mid = r1.astype(BF16)
    lo = (r1 - mid.astype(F32)).astype(BF16)
    return hi, mid, lo


def _exact_left(m_bf16, x):
    hi, mid, lo = _split3(x)
    dot = lambda p: jnp.dot(m_bf16, p, preferred_element_type=F32)
    return dot(hi) + dot(mid) + dot(lo)


def _exact_tn(x, m_bf16):
    hi, mid, lo = _split3(x)
    dot = lambda p: lax.dot_general(p, m_bf16, (((0,), (0,)), ((), ())), preferred_element_type=F32)
    return dot(hi) + dot(mid) + dot(lo)


def _rms(x, gain):
    return x * lax.rsqrt(jnp.mean(x * x, axis=-1, keepdims=True) + RMS_EPS) * gain


def _softplus(x):
    return jnp.maximum(x, 0.0) + jnp.log1p(jnp.exp(-jnp.abs(x)))


def _sigmoid(x):
    return 1.0 / (1.0 + jnp.exp(-x))


def _silu(x):
    return x * _sigmoid(x)


def _iota(shape, dim):
    return lax.broadcasted_iota(jnp.int32, shape, dim)


def _nilpotent_inverse(ns):
    eye = _iota((128, 128), 0) == _iota((128, 128), 1)
    plus_eye = lambda m: jnp.where(eye, 1.0, m).astype(BF16)
    inv = [plus_eye(n) for n in ns]
    nb = [n.astype(BF16) for n in ns]
    pw = [_mm(m, m) for m in nb]
    for step in range(5):
        rhs = [plus_eye(m) for m in pw]
        if step < 4:
            pwb = [m.astype(BF16) for m in pw]
            both = [_mm(jnp.concatenate([t, m], axis=0), r) for t, m, r in zip(inv, pwb, rhs)]
            inv = [b[:128].astype(BF16) for b in both]
            pw = [b[128:] - m.astype(F32) for b, m in zip(both, pwb)]
        else:
            inv = [_mm(t, r).astype(BF16) for t, r in zip(inv, rhs)]
    return inv


def _chunk_rows(j):
    return slice(j * CHUNK, (j + 1) * CHUNK)


def _is_first_step():
    return pl.program_id(1) == 0


def _is_last_step():
    return pl.program_id(1) == pl.num_programs(1) - 1


def _split_specs(npt, rows, width):
    return [pl.BlockSpec((rows, width), lambda i: (jnp.minimum(i, npt - 1), 0)),
            pl.BlockSpec((rows, width), lambda i: (jnp.maximum(i - npt, 0), 0))]


def _x_specs(x, npt):
    if len(x) == 1:
        return [pl.BlockSpec((TOKEN_TILE, D_MODEL), lambda i: (i, 0))]
    return _split_specs(npt, TOKEN_TILE, D_MODEL)


def _n_tiles(x):
    rows = sum(a.shape[0] for a in x)
    assert all(a.shape[0] % TOKEN_TILE == 0 for a in x)
    return rows // TOKEN_TILE


def _read_tile(in_prompt, refs):
    return refs[0][...] if len(refs) == 1 else jnp.where(in_prompt, refs[0][...], refs[1][...])


def _store_tile(in_prompt, p_ref, s_ref, value):
    @pl.when(in_prompt)
    def _():
        p_ref[...] = value

    @pl.when(jnp.logical_not(in_prompt))
    def _():
        s_ref[...] = value


def _proj_body(*refs, n, n_prompt_tiles, nx):
    gain_ref, w_ref, o_ref = refs[nx:]
    x = _read_tile(pl.program_id(0) < n_prompt_tiles, refs[:nx])
    h = _rms(x, gain_ref[...]).astype(BF16)
    for lo in range(0, n, PROJ_COLS):
        cols = slice(lo, min(lo + PROJ_COLS, n))
        o_ref[:, cols] = jnp.dot(h, w_ref[:, cols], preferred_element_type=F32)


def _norm_proj(x, npt, gain, w):
    n = w.shape[1]
    tm = TOKEN_TILE
    tiles = _n_tiles(x)
    return pl.pallas_call(
        functools.partial(_proj_body, n=n, n_prompt_tiles=npt, nx=len(x)),
        grid=(tiles,),
        in_specs=_x_specs(x, npt) + [pl.BlockSpec((1, D_MODEL), lambda i: (0, 0)),
                                     pl.BlockSpec((D_MODEL, n), lambda i: (0, 0))],
        out_specs=pl.BlockSpec((tm, n), lambda i: (i, 0)),
        out_shape=jax.ShapeDtypeStruct((tiles * tm, n), F32),
        compiler_params=pltpu.CompilerParams(
            dimension_semantics=("arbitrary",), vmem_limit_bytes=VMEM_LIMIT),
        name="norm_proj",
    )(*x, gain, w)


def _ffn_body(*refs, n_prompt_tiles, final, nx):
    yp_ref, ys_ref, wo_ref, gain_ref, win_ref, wout_ref, fgain_ref = refs[nx:nx + 7]
    acc_ref = refs[-1]
    in_prompt = pl.program_id(0) < n_prompt_tiles
    x = _read_tile(in_prompt, refs[:nx])
    y = _read_tile(in_prompt, (yp_ref, ys_ref))
    x1 = x + jnp.dot(y.astype(BF16), wo_ref[...], preferred_element_type=F32)
    h = _rms(x1, gain_ref[...]).astype(BF16)
    acc_ref[...] = x1
    for c in range(D_FF // FFN_COLS):
        lo = c * FFN_COLS
        g = jnp.dot(h, win_ref[:, lo:lo + FFN_COLS], preferred_element_type=F32)
        u = jnp.dot(h, win_ref[:, D_FF + lo:D_FF + lo + FFN_COLS], preferred_element_type=F32)
        a = (_silu(g) * u).astype(BF16)
        acc_ref[...] += jnp.dot(a, wout_ref[lo:lo + FFN_COLS, :], preferred_element_type=F32)
    if final:
        _store_tile(in_prompt, refs[nx + 7], refs[nx + 8], _rms(acc_ref[...], fgain_ref[...]))
    else:
        refs[nx + 7][...] = acc_ref[...]


def _out_ffn(x, y_prompt, y_sample, wo, gain, win, wout, layer, fgain, final):
    tm = TOKEN_TILE
    tiles = _n_tiles(x)
    npt = y_prompt.shape[0] // tm
    fixed = lambda i: (0, 0)
    of_layer = lambda i: (layer, 0, 0)
    if final:
        out_specs = _split_specs(npt, tm, D_MODEL)
        out_shape = [jax.ShapeDtypeStruct((y.shape[0], D_MODEL), F32) for y in (y_prompt, y_sample)]
    else:
        out_specs = [pl.BlockSpec((tm, D_MODEL), lambda i: (i, 0))]
        out_shape = [jax.ShapeDtypeStruct((tiles * tm, D_MODEL), F32)]
    return pl.pallas_call(
        functools.partial(_ffn_body, n_prompt_tiles=npt, final=final, nx=len(x)),
        grid=(tiles,),
        in_specs=_x_specs(x, npt) + _split_specs(npt, tm, D_MODEL) + [
            pl.BlockSpec((D_MODEL, D_MODEL), fixed),
            pl.BlockSpec((1, D_MODEL), fixed),
            pl.BlockSpec((None, D_MODEL, 2 * D_FF), of_layer),
            pl.BlockSpec((None, D_FF, D_MODEL), of_layer),
            pl.BlockSpec((1, D_MODEL), fixed)],
        out_specs=out_specs,
        out_shape=out_shape,
        scratch_shapes=[pltpu.VMEM((tm, D_MODEL), F32)],
        compiler_params=pltpu.CompilerParams(
            dimension_semantics=("arbitrary",), vmem_limit_bytes=VMEM_LIMIT),
        name="out_ffn",
    )(*x, y_prompt, y_sample, wo, gain, win, wout, fgain)


def _a_proj_body(*refs, tm, n_prompt_tiles, npch, cps, cpd, nx):
    (sh_ref, gain_ref, mu_ref, w0_ref, a0_ref, wr_ref, wk_ref, wv_ref, w1_ref, a1_ref, g1_ref,
     w2_ref, a2_ref, g2_ref, r_ref, k_ref, v_ref, lw_ref, al_ref, gg_ref, hl_ref) = refs[2 * nx:]
    i = pl.program_id(0)
    in_prompt = i < n_prompt_tiles
    gain = gain_ref[...]
    h = _rms(_read_tile(in_prompt, refs[:nx]), gain)
    prev_tile_last = _rms(_read_tile(in_prompt, refs[nx:2 * nx]), gain)[7:8, :]
    row = _iota((CHUNK, 1), 0)
    pieces = []
    for c in range(tm // CHUNK):
        hc = h[_chunk_rows(c)]
        natural = prev_tile_last if c == 0 else h[c * CHUNK - 1:c * CHUNK]
        gc = i * (tm // CHUNK) + c
        start = jnp.where(gc < npch, gc % cps == 0, (gc - npch) % cpd == 0)
        first = jnp.where(start, sh_ref[c:c + 1, :], natural)
        pieces.append(jnp.where(row == 0, first, pltpu.roll(hc, 1, 0)))
        hl_ref[c:c + 1, :] = hc[CHUNK - 1:CHUNK]
    xx = jnp.concatenate(pieces, axis=0) - h
    mix = lambda j: (h + xx * mu_ref[j:j + 1, :]).astype(BF16)
    dot = lambda a, w_ref: jnp.dot(a, w_ref[...], preferred_element_type=F32)
    r_ref[...] = dot(mix(0), wr_ref)
    z = w0_ref[...] + _mm(jnp.tanh(dot(mix(1), w1_ref)), w2_ref[...])
    lw_ref[...] = (-DECAY_SCALE) * _sigmoid(z)
    k_ref[...] = dot(mix(2), wk_ref)
    v_ref[...] = dot(mix(3), wv_ref)
    al_ref[...] = _sigmoid(a0_ref[...] + _mm(dot(mix(4), a1_ref), a2_ref[...]))
    gg_ref[...] = _mm(_sigmoid(dot(mix(5), g1_ref)), g2_ref[...])


def _a_proj(x, npt, shift_rows, gain, mu, w0, a0, wr, wk, wv, w1, a1, g1, w2, a2, g2, npch, cps, cpd):
    tm = TOKEN_TILE
    tiles = _n_tiles(x)
    nst = tiles - npt
    nt = tiles * tm
    row = lambda i: (i, 0)
    fixed = lambda i: (0, 0)
    full = lambda a: pl.BlockSpec(a.shape, fixed)
    tok = pl.BlockSpec((tm, D_MODEL), row)
    tok_shape = jax.ShapeDtypeStruct((nt, D_MODEL), F32)
    per8 = tm // 8
    if len(x) == 1:
        before = [pl.BlockSpec((8, D_MODEL), lambda i: (jnp.maximum(i * per8 - 1, 0), 0))]
    else:
        before = [pl.BlockSpec((8, D_MODEL), lambda i: (jnp.clip(i * per8 - 1, 0, npt * per8 - 1), 0)),
                  pl.BlockSpec((8, D_MODEL), lambda i: (jnp.clip((i - npt) * per8 - 1, 0, nst * per8 - 1), 0))]
    return pl.pallas_call(
        functools.partial(_a_proj_body, tm=tm, n_prompt_tiles=npt, npch=npch, cps=cps, cpd=cpd, nx=len(x)),
        grid=(tiles,),
        in_specs=_x_specs(x, npt) + before + [
            pl.BlockSpec((tm // CHUNK, D_MODEL), row),
            full(gain), full(mu), full(w0), full(a0), full(wr), full(wk), full(wv),
            full(w1), full(a1), full(g1), full(w2), full(a2), full(g2)],
        out_specs=[tok] * 6 + [pl.BlockSpec((tm // CHUNK, D_MODEL), row)],
        out_shape=[tok_shape] * 6 + [jax.ShapeDtypeStruct((nt // CHUNK, D_MODEL), F32)],
        compiler_params=pltpu.CompilerParams(
            dimension_semantics=("arbitrary",), vmem_limit_bytes=VMEM_LIMIT),
        name="rwkv_proj",
    )(*x, *x, shift_rows, gain, mu, w0, a0, wr, wk, wv, w1, a1, g1, w2, a2, g2)


def _a_chunk_body(r_ref, k_ref, v_ref, lw_ref, al_ref, gg_ref, kk_ref, ka_ref, rk_ref, lnw_ref, lnb_ref,
                  *rest, nchunks, carry):
    if carry:
        y_ref, sout_ref, ss_ref = rest
    else:
        sin_ref, y_ref, sout_ref = rest

    lane = _iota((1, 128), 1)
    head0 = lane < A_HEAD
    ri = _iota((128, 128), 0) % CHUNK
    ci = _iota((128, 128), 1) % CHUNK
    strict = ri > ci
    incl = ri >= ci
    tri = (_iota((CHUNK, CHUNK), 0) >= _iota((CHUNK, CHUNK), 1)).astype(BF16)
    zero = jnp.zeros((A_HEAD, A_HEAD), F32)
    same_head = (_iota((128, 128), 0) // A_HEAD) == (_iota((128, 128), 1) // A_HEAD)
    head_mean = jnp.where(same_head, 1.0 / A_HEAD, 0.0).astype(BF16)

    def headsum(x):
        s0 = jnp.sum(jnp.where(head0, x, 0.0), axis=-1, keepdims=True)
        s1 = jnp.sum(jnp.where(head0, 0.0, x), axis=-1, keepdims=True)
        return jnp.where(head0, s0, s1)

    def expand(x):
        xb = x.astype(BF16)
        zb = jnp.zeros_like(xb)
        return jnp.concatenate([jnp.where(head0, xb, zb), jnp.where(head0, zb, xb)], axis=0)

    def load_state(ref, j):
        return [jnp.concatenate([jnp.concatenate([ref[j, 2 * p], zero], axis=1),
                                 jnp.concatenate([zero, ref[j, 2 * p + 1]], axis=1)], axis=0)
                for p in range(A_PAIRS)]

    def store_state(ref, j, state):
        for p in range(A_PAIRS):
            ref[j, 2 * p] = state[p][:A_HEAD, :A_HEAD]
            ref[j, 2 * p + 1] = state[p][A_HEAD:, A_HEAD:]

    pairs = range(A_PAIRS)
    sls = [slice(p * 128, (p + 1) * 128) for p in pairs]
    units = [(j, p) for j in range(nchunks) for p in pairs]
    gcum_all = [_exact_left(tri, lw_ref[_chunk_rows(j), :]) for j in range(nchunks)]
    xs, ys, ves, ends, glasts, bonus_sums = [], [], [], [], [], []
    for j, p in units:
        rs, sl = _chunk_rows(j), sls[p]
        k = k_ref[rs, sl]
        lw = lw_ref[rs, sl]
        al = al_ref[rs, sl]
        r = r_ref[rs, sl]
        kk = k * kk_ref[:, sl]
        kk = kk * lax.rsqrt(jnp.maximum(headsum(kk * kk), 1e-24))
        kmod = k * (1.0 + (al - 1.0) * ka_ref[:, sl])
        gcum = gcum_all[j][:, sl]
        glast = gcum[CHUNK - 1:CHUNK, :]
        dec_out = jnp.exp(-gcum)
        dec_end = jnp.exp(glast - gcum)
        ae = expand(-kk * jnp.exp(gcum - lw))
        re = expand(r * jnp.exp(gcum))
        be = expand(kk * al * dec_out)
        ke = expand(kmod * dec_out)
        xs.append(jnp.concatenate([ae, re], axis=0))
        ys.append(jnp.concatenate([be, ke], axis=0))
        ves.append(expand(v_ref[rs, sl]))
        ends.append(jnp.concatenate([expand(kk * al * dec_end), expand(kmod * dec_end)], axis=0))
        glasts.append(glast)
        bonus_sums.append(headsum(r * kmod * rk_ref[:, sl]))

    sc = [_mm_nt(x, y) for x, y in zip(xs, ys)]
    n_ab = [jnp.where(strict, s[:128, :128], 0.0) for s in sc]
    l_ak = [jnp.where(strict, s[:128, 128:], 0.0).astype(BF16) for s in sc]
    m_rb = [jnp.where(incl, s[128:, :128], 0.0).astype(BF16) for s in sc]
    m_rk = [jnp.where(incl, s[128:, 128:], 0.0).astype(BF16) for s in sc]
    tinv = _nilpotent_inverse(n_ab)
    lv = [_mm(l, ve).astype(BF16) for l, ve in zip(l_ak, ves)]
    pq = [_mm(t, jnp.concatenate([x[:128], l], axis=1)).astype(BF16)
          for t, x, l in zip(tinv, xs, lv)]
    lhs = [jnp.concatenate([(x[128:] + _mm(m, z[:, :128])).astype(BF16), z[:, :128]], axis=0)
           for x, m, z in zip(xs, m_rb, pq)]
    o_intra = [_mm(jnp.concatenate([mb, mk], axis=1), jnp.concatenate([z[:, 128:], ve], axis=0))
               for mb, mk, z, ve in zip(m_rb, m_rk, pq, ves)]

    if carry:
        @pl.when(_is_first_step())
        def _():
            ss_ref[...] = jnp.zeros_like(ss_ref)

        state = [ss_ref[p] for p in pairs]
    for j in range(nchunks):
        if not carry:
            state = load_state(sin_ref, j)
        un = [j * A_PAIRS + p for p in pairs]
        ou = [_mm_nt(lhs[u], s) for u, s in zip(un, state)]
        oe = [o[:128] + o_intra[u] for o, u in zip(ou, un)]
        uv = [jnp.concatenate([(o[128:] + pq[u][:, 128:]).astype(BF16), ves[u]], axis=0) for o, u in zip(ou, un)]
        upd = [_mm_tn(a, ends[u]) for a, u in zip(uv, un)]
        state = [s * jnp.exp(glasts[u]) + d for s, u, d in zip(state, un, upd)]
        if not carry:
            store_state(sout_ref, j, state)
        rs = _chunk_rows(j)
        os = [o[:CHUNK] + o[CHUNK:] for o in oe]
        parts = []
        for o in os:
            m = jnp.concatenate([o, o * o], axis=0)
            hi = m.astype(BF16)
            parts.append(jnp.concatenate([hi, (m - hi.astype(F32)).astype(BF16)], axis=0))
        mom = [jnp.dot(m, head_mean, preferred_element_type=F32) for m in parts]
        for p, sl in enumerate(sls):
            mean = mom[p][:CHUNK] + mom[p][2 * CHUNK:3 * CHUNK]
            var = mom[p][CHUNK:2 * CHUNK] + mom[p][3 * CHUNK:] - mean * mean
            on = (os[p] - mean) * lax.rsqrt(var + A_LNX_EPS) * lnw_ref[:, sl] + lnb_ref[:, sl]
            y_ref[rs, sl] = (on + bonus_sums[un[p]] * v_ref[rs, sl]) * gg_ref[rs, sl]
    if carry:
        for p in pairs:
            ss_ref[p] = state[p]

        @pl.when(_is_last_step())
        def _():
            store_state(sout_ref, 0, [ss_ref[p] for p in pairs])


def _chunk_calls(body, name, k, prompt_in, sample_in, shared, state_in, state_shapes, y_width, scratch,
                 bp, seq, bd, dseq, layer=None):
    assert dseq == CHUNK, "sample streams are processed as single chunks"
    rows = k * CHUNK
    steps = seq // rows
    params = pltpu.CompilerParams(dimension_semantics=("arbitrary",) * 2, vmem_limit_bytes=VMEM_LIMIT)

    def run(carry, grid, tok_index, col_specs, states, out_rows, nstate, state_index):
        ndim = len(grid)
        fixed = lambda a: pl.BlockSpec(a.shape, lambda *g: (0,) * a.ndim)
        tok_specs = [pl.BlockSpec((rows, w), functools.partial(lambda c, *g: (tok_index(*g), c), c))
                     for (_, w, c) in col_specs]
        st_specs = [pl.BlockSpec((nstate,) + s, functools.partial(lambda n, *g: (state_index(*g),) + (0,) * n, len(s)))
                    for s in state_shapes]
        st_in_specs = st_specs if layer is None else [
            pl.BlockSpec((None, nstate) + s,
                         functools.partial(lambda n, *g: (layer, state_index(*g)) + (0,) * n, len(s)))
            for s in state_shapes]
        y_spec = pl.BlockSpec((rows, y_width), lambda *g: (tok_index(*g) - tok_index(*(0,) * ndim), 0))
        return pl.pallas_call(
            functools.partial(body, nchunks=k, carry=carry),
            grid=grid,
            in_specs=tok_specs + [fixed(a) for a in shared] + (st_in_specs if states else []),
            out_specs=[y_spec] + st_specs,
            out_shape=[jax.ShapeDtypeStruct((out_rows, y_width), F32)]
            + [jax.ShapeDtypeStruct((grid[0] * nstate if not carry else grid[0],) + s, F32) for s in state_shapes],
            scratch_shapes=scratch if carry else [],
            compiler_params=params if ndim == 2 else pltpu.CompilerParams(
                dimension_semantics=("arbitrary",), vmem_limit_bytes=VMEM_LIMIT),
            name=name + ("_prompt" if carry else "_sample"),
        )(*[a for (a, _, _) in col_specs], *shared, *states)

    prompt = run(True, (bp, steps), lambda b, i: b * steps + i, prompt_in, [], bp * seq, 1, lambda b, i: b)
    off = bp * seq // rows
    sample = run(False, (bd // k,), lambda i: off + i, sample_in, state_in, bd * dseq, k, lambda i: i)
    return prompt, sample


def _head_rms_gate(o, gain, gate):
    return o * lax.rsqrt(jnp.mean(o * o, axis=-1, keepdims=True) + RMS_EPS) * gain * _silu(gate)


def _b_chunk_body(q_ref, k_ref, v_ref, gate_ref, lo_ref, wa2_ref, ba_ref, onorm_ref, *rest, nchunks, carry):
    if carry:
        y_ref, sout_ref, ss_ref = rest
    else:
        sin_ref, y_ref, sout_ref = rest
    incl = _iota((CHUNK, CHUNK), 0) >= _iota((CHUNK, CHUNK), 1)
    tri = incl.astype(BF16)
    heads = range(B_HEADS)
    kss = [slice(h * B_DK, (h + 1) * B_DK) for h in heads]
    vss = [slice(h * B_DV, (h + 1) * B_DV) for h in heads]
    units = [(j, h) for j in range(nchunks) for h in heads]
    gk_all = -_softplus(-(_mm(lo_ref[...], wa2_ref[...]) + ba_ref[...])) * (1.0 / B_GATE_TAU)
    b_all = [_exact_left(tri, gk_all[_chunk_rows(j)]) for j in range(nchunks)]
    qt, kt, khat, blast = [], [], [], []
    for j, h in units:
        rs, ks = _chunk_rows(j), kss[h]
        b = b_all[j][:, ks]
        k = k_ref[rs, ks]
        blast.append(b[CHUNK - 1:CHUNK, :])
        qt.append(q_ref[rs, ks] * (B_DK ** -0.5) * jnp.exp(b))
        kt.append(k * jnp.exp(-b))
        khat.append(k * jnp.exp(blast[-1] - b))
    a = [jnp.where(incl, _mm_nt(x, y), 0.0) for x, y in zip(qt, kt)]
    o_intra = [_mm(a[u], v_ref[_chunk_rows(j), vss[h]]) for u, (j, h) in enumerate(units)]
    upd = [_mm_tn(v_ref[_chunk_rows(j), vss[h]], khat[u]) for u, (j, h) in enumerate(units)]

    if carry:
        @pl.when(_is_first_step())
        def _():
            ss_ref[...] = jnp.zeros_like(ss_ref)

        state = [ss_ref[h] for h in heads]
    for j in range(nchunks):
        if not carry:
            state = [sin_ref[j, h].T for h in heads]
        rs = _chunk_rows(j)
        o = [o_intra[j * B_HEADS + h] + _mm_nt(qt[j * B_HEADS + h], state[h]) for h in heads]
        state = [state[h] * jnp.exp(blast[j * B_HEADS + h]) + upd[j * B_HEADS + h] for h in heads]
        for h in heads:
            if not carry:
                sout_ref[j, h] = state[h].T
            y_ref[rs, vss[h]] = _head_rms_gate(o[h], onorm_ref[...], gate_ref[rs, vss[h]])
    if carry:
        for h in heads:
            ss_ref[h] = state[h]

        @pl.when(_is_last_step())
        def _():
            for h in heads:
                sout_ref[0, h] = ss_ref[h].T


def _c_chunk_body(qkv_ref, z_ref, ba_ref, cw_ref, alog_ref, dtb_ref, onorm_ref, *rest, nchunks, carry):
    if carry:
        y_ref, cvout_ref, sout_ref, pv_ref, ss_ref = rest
    else:
        cvin_ref, sin_ref, y_ref, cvout_ref, sout_ref = rest

    def conv_act(x, pv):
        row8 = _iota((8, 1), 0)
        conv = x * cw_ref[C_CONV - 1:C_CONV, :]
        for j in range(1, C_CONV):
            sh = pltpu.roll(x, j, 0)
            head = jnp.where(row8 < j, pltpu.roll(pv, j, 0), sh[:8])
            sh = jnp.concatenate([head, sh[8:]], axis=0)
            conv = conv + sh * cw_ref[C_CONV - 1 - j:C_CONV - j, :]
        return _silu(conv)

    if carry:
        @pl.when(_is_first_step())
        def _():
            ss_ref[...] = jnp.zeros_like(ss_ref)
            pv_ref[...] = jnp.zeros_like(pv_ref)

        x = qkv_ref[...]
        act = conv_act(x, pv_ref[...])
        pv_ref[...] = x[nchunks * CHUNK - 8:]
    else:
        acts = []
        for j in range(nchunks):
            x = qkv_ref[_chunk_rows(j), :]
            acts.append(conv_act(x, cvin_ref[j]))
            cvout_ref[j] = x[CHUNK - 8:]
        act = jnp.concatenate(acts, axis=0)

    ba = ba_ref[...]
    beta_all = _sigmoid(ba)
    g_all = -jnp.exp(alog_ref[...]) * _softplus(ba + dtb_ref[...])
    tri = (_iota((CHUNK, CHUNK), 0) >= _iota((CHUNK, CHUNK), 1)).astype(BF16)
    triu2 = (_iota((CHUNK, 128), 0) <= _iota((CHUNK, 128), 1) % CHUNK).astype(BF16)
    ri = _iota((128, 128), 0)
    ci = _iota((128, 128), 1)
    same = (ri // CHUNK) == (ci // CHUNK)
    strict = same & (ri % CHUNK > ci % CHUNK)
    incl = same & (ri % CHUNK >= ci % CHUNK)
    lane_lo = _iota((1, 128), 1) < CHUNK
    zeros = jnp.zeros((CHUNK, C_DK), BF16)
    l2 = lambda t: t * lax.rsqrt(jnp.sum(t * t, axis=-1, keepdims=True) + 1e-6)
    blockdiag = lambda a, b: jnp.concatenate(
        [jnp.concatenate([a.astype(BF16), zeros], axis=1),
         jnp.concatenate([zeros, b.astype(BF16)], axis=1)], axis=0)

    heads = range(C_HEADS)
    npairs = C_HEADS // 2
    q, k, v, beta, kb, col, egc, glast = ({} for _ in range(8))
    gc_row = []
    for j in range(nchunks):
        rs = _chunk_rows(j)
        gc_col = _exact_left(tri, g_all[rs])
        gc_row.append(_exact_tn(g_all[rs], triu2))
        for h in heads:
            u = (j, h)
            q[u] = l2(act[rs, h * C_DK:(h + 1) * C_DK]) * (C_DK ** -0.5)
            k[u] = l2(act[rs, C_HEADS * C_DK + h * C_DK:C_HEADS * C_DK + (h + 1) * C_DK])
            v[u] = act[rs, 2 * C_HEADS * C_DK + h * C_DV:2 * C_HEADS * C_DK + (h + 1) * C_DV]
            beta[u] = beta_all[rs, h:h + 1]
            kb[u] = k[u] * beta[u]
            col[u] = gc_col[:, C_HEADS + h:C_HEADS + h + 1]
            egc[u] = jnp.exp(col[u])
            glast[u] = col[u][CHUNK - 1:CHUNK, :]
    units = [(j, p) for j in range(nchunks) for p in range(npairs)]
    decay, sc, rhs = [], [], []
    for j, p in units:
        a, b = (j, 2 * p), (j, 2 * p + 1)
        col_pair = jnp.concatenate([col[a], col[b]], axis=0)
        row_pair = jnp.where(lane_lo, gc_row[j][C_HEADS + 2 * p:C_HEADS + 2 * p + 1, :],
                             gc_row[j][C_HEADS + 2 * p + 1:C_HEADS + 2 * p + 2, :])
        decay.append(jnp.where(incl, jnp.exp(jnp.where(incl, col_pair - row_pair, 0.0)), 0.0))
        lhs = jnp.concatenate([blockdiag(kb[a], kb[b]), blockdiag(q[a], q[b])], axis=0)
        sc.append(_mm_nt(lhs, blockdiag(k[a], k[b])))
        rhs.append(jnp.concatenate(
            [jnp.concatenate([v[u] * beta[u], kb[u] * egc[u]], axis=1) for u in (a, b)], axis=0).astype(BF16))
    n_l = [-jnp.where(strict, s[:128], 0.0) * d for s, d in zip(sc, decay)]
    attn = [(s[128:] * d).astype(BF16) for s, d in zip(sc, decay)]
    tinv = _nilpotent_inverse(n_l)
    sol = [_mm(t, r) for t, r in zip(tinv, rhs)]

    if carry:
        state = [ss_ref[h] for h in heads]
    for j in range(nchunks):
        if not carry:
            state = [sin_ref[j, h] for h in heads]
        rs = _chunk_rows(j)
        half = lambda m, h: m[(h % 2) * CHUNK:(h % 2 + 1) * CHUNK]
        qw = [_mm(jnp.concatenate([q[j, h] * egc[j, h], half(sol[j * npairs + h // 2], h)[:, C_DV:]], axis=0),
                  state[h]) for h in heads]
        v_new = [(sol[j * npairs + p][:, :C_DV]
                  - jnp.concatenate([qw[2 * p][CHUNK:], qw[2 * p + 1][CHUNK:]], axis=0)).astype(BF16)
                 for p in range(npairs)]
        o_intra = [_mm(attn[j * npairs + p], v_new[p]) for p in range(npairs)]
        upd = [_mm_tn(k[j, h] * jnp.exp(glast[j, h] - col[j, h]), half(v_new[h // 2], h)) for h in heads]
        state = [state[h] * jnp.exp(glast[j, h]) + upd[h] for h in heads]
        for h in heads:
            o = qw[h][:CHUNK] + half(o_intra[h // 2], h)
            vs = slice(h * C_DV, (h + 1) * C_DV)
            y_ref[rs, vs] = _head_rms_gate(o, onorm_ref[...], z_ref[rs, vs])
            if not carry:
                sout_ref[j, h] = state[h]
    if carry:
        for h in heads:
            ss_ref[h] = state[h]

        @pl.when(_is_last_step())
        def _():
            sout_ref[0] = ss_ref[...]
            cvout_ref[0] = pv_ref[...]


def _pad_cols(w, n):
    return jnp.pad(w, ((0, 0), (0, n - w.shape[1])))


def kernel(x_prompt, x_sample, state_a_shift, state_a_wkv, state_b_kv, state_c_conv, state_c_kv, norm_mix, norm_ffn, norm_final, ffn_w_in, ffn_w_out, a_mu, a_w0, a_w1, a_w2, a_a0, a_a1, a_a2, a_g1, a_g2, a_k_k, a_k_a, a_r_k, a_w_rkv, a_w_o, a_lnx_w, a_lnx_b, b_w_in, b_w_a1, b_w_a2, b_b_a, b_onorm, b_w_o, c_w_in, c_conv_w, c_a_log, c_dt_bias, c_onorm, c_w_o):
    bp, seq, d = x_prompt.shape
    bd, dseq, _ = x_sample.shape
    depth = norm_mix.shape[0]
    cps, cpd = seq // CHUNK, dseq // CHUNK
    npch = bp * cps
    shape = (bp, seq, bd, dseq)
    x = (x_prompt.reshape(bp * seq, d), x_sample.reshape(bd * dseq, d))
    npt = bp * seq // TOKEN_TILE
    n_chunks = (bp * seq + bd * dseq) // CHUNK
    row = lambda a: a.reshape(1, -1)
    bf = lambda a: a.astype(BF16)
    ffn_in, ffn_out = bf(ffn_w_in), bf(ffn_w_out)

    last_prompt = [(s + 1) * cps - 1 for s in range(bp)]
    last_sample = [npch + (s + 1) * cpd - 1 for s in range(bd)]

    out_a_shift, out_a_wkv, out_b_kv, out_c_conv, out_c_kv = [], [], [], [], []
    for i in range(depth):
        j = i // 3
        gain = row(norm_mix[i])
        if i % 3 == 0:
            shift_rows = jnp.zeros((n_chunks, d), F32)
            shift_rows = shift_rows.at[jnp.asarray([npch + s * cpd for s in range(bd)])].set(state_a_shift[j])
            proj = _a_proj(
                x, npt, shift_rows, gain, a_mu[j], row(a_w0[j]), row(a_a0[j]),
                bf(a_w_rkv[j, 0]), bf(a_w_rkv[j, 1]), bf(a_w_rkv[j, 2]),
                bf(a_w1[j]), bf(a_a1[j]), bf(a_g1[j]), bf(a_w2[j]), bf(a_a2[j]), bf(a_g2[j]), npch, cps, cpd)
            hl = proj[6]
            cols = [(a, D_MODEL, 0) for a in proj[:6]]
            shared = [row(a_k_k[j]), row(a_k_a[j]), row(a_r_k[j]), row(a_lnx_w[j]), row(a_lnx_b[j])]
            (y_p, wkv_p), (y_s, wkv_s) = _chunk_calls(
                _a_chunk_body, "rwkv_chunk", A_CHUNKS_PER_STEP, cols, cols, shared, [state_a_wkv],
                [(A_HEADS, A_HEAD, A_HEAD)], D_MODEL, [pltpu.VMEM((A_PAIRS, 128, 128), F32)], *shape, layer=j)
            out_a_shift.append((hl[jnp.asarray(last_prompt)], hl[jnp.asarray(last_sample)]))
            out_a_wkv.append((wkv_p, wkv_s))
            w_o = a_w_o[j]
        elif i % 3 == 1:
            w = jnp.concatenate([b_w_in[j], _pad_cols(b_w_a1[j], B_LORA_PAD)], axis=1)
            proj = _norm_proj(x, npt, gain, bf(w))
            wa2 = jnp.pad(b_w_a2[j], ((0, B_LORA_PAD - b_w_a2.shape[1]), (0, 0)))
            qk_w, v_w = B_HEADS * B_DK, B_HEADS * B_DV
            cols = [(proj, qk_w, 0), (proj, qk_w, 1), (proj, v_w, 1), (proj, v_w, 2),
                    (proj, B_LORA_PAD, (2 * qk_w + 2 * v_w) // B_LORA_PAD)]
            shared = [bf(wa2), row(b_b_a[j]), row(b_onorm[j])]
            (y_p, kv_p), (y_s, kv_s) = _chunk_calls(
                _b_chunk_body, "gla_chunk", B_CHUNKS_PER_STEP, cols, cols, shared, [state_b_kv[j]],
                [(B_HEADS, B_DK, B_DV)], v_w, [pltpu.VMEM((B_HEADS, B_DV, B_DK), F32)], *shape)
            out_b_kv.append((kv_p, kv_s))
            w_o = b_w_o[j]
        else:
            proj = _norm_proj(x, npt, gain, bf(_pad_cols(c_w_in[j], C_PROJ_PAD)))
            lanes = lambda a: jnp.pad(a.reshape(1, -1), ((0, 0), (C_HEADS, 128 - 2 * C_HEADS)))
            z_w = C_HEADS * C_DV
            cols = [(proj, C_QKV, 0), (proj, z_w, C_QKV // z_w), (proj, 128, (C_QKV + z_w) // 128)]
            shared = [c_conv_w[j], lanes(c_a_log[j]), lanes(c_dt_bias[j]), row(c_onorm[j])]
            cv_in = jnp.pad(state_c_conv[j], ((0, 0), (8 - (C_CONV - 1), 0), (0, 0)))
            (y_p, cv_p, kv_p), (y_s, cv_s, kv_s) = _chunk_calls(
                _c_chunk_body, "gdn_chunk", C_CHUNKS_PER_STEP, cols, cols, shared, [cv_in, state_c_kv[j]],
                [(8, C_QKV), (C_HEADS, C_DK, C_DV)], z_w,
                [pltpu.VMEM((8, C_QKV), F32), pltpu.VMEM((C_HEADS, C_DK, C_DV), F32)], *shape)
            out_c_conv.append((cv_p[:, 8 - (C_CONV - 1):], cv_s[:, 8 - (C_CONV - 1):]))
            out_c_kv.append((kv_p, kv_s))
            w_o = c_w_o[j]
        final = i == depth - 1
        x = tuple(_out_ffn(x, y_p, y_s, bf(w_o), row(norm_ffn[i]), ffn_in, ffn_out, i,
                           row(norm_final), final))

    split = lambda parts: (jnp.stack([p for p, _ in parts]), jnp.stack([s for _, s in parts]))
    a_shift_p, a_shift_s = split(out_a_shift)
    a_wkv_p, a_wkv_s = split(out_a_wkv)
    b_kv_p, b_kv_s = split(out_b_kv)
    c_conv_p, c_conv_s = split(out_c_conv)
    c_kv_p, c_kv_s = split(out_c_kv)
    y_prompt = x[0].reshape(bp, seq, d)
    y_sample = x[1].reshape(bd, dseq, d)
    return (y_prompt, y_sample, a_shift_p, a_wkv_p, b_kv_p, c_conv_p, c_kv_p,
            a_shift_s, a_wkv_s, b_kv_s, c_conv_s, c_kv_s)
```

```python
import functools
import math

import jax
import jax.numpy as jnp
from jax import lax
from jax.experimental import pallas as pl
from jax.experimental.pallas import tpu as pltpu

F32 = jnp.float32
BF16 = jnp.bfloat16

D_MODEL = 1024
CHUNK = 64
RMS_EPS = 1e-6
D_FF = 2816
FFN_COLS = 256
PROJ_COLS = 512

A_HEADS = 16
A_HEAD = 64
A_PAIRS = A_HEADS // 2
A_LNX_EPS = 64e-5
DECAY_SCALE = math.exp(-0.5)

B_HEADS = 4
B_DK = 128
B_DV = 256
B_GATE_TAU = 16.0
B_LORA_PAD = 128

C_HEADS = 8
C_DK = 128
C_DV = 128
C_CONV = 4
C_QKV = C_HEADS * (2 * C_DK + C_DV)
C_PROJ_PAD = C_QKV + C_HEADS * C_DV + 128

TOKEN_TILE = 512
A_CHUNKS_PER_STEP = 4
B_CHUNKS_PER_STEP = 8
C_CHUNKS_PER_STEP = 4
VMEM_LIMIT = 56 * 1024 * 1024


def _mm(a, b):
    return jnp.dot(a.astype(BF16), b.astype(BF16), preferred_element_type=F32)


def _mm_nt(a, b):
    return lax.dot_general(a.astype(BF16), b.astype(BF16), (((1,), (1,)), ((), ())),
                           preferred_element_type=F32)


def _mm_tn(a, b):
    return lax.dot_general(a.astype(BF16), b.astype(BF16), (((0,), (0,)), ((), ())),
                           preferred_element_type=F32)


def _split3(x):
    hi = x.astype(BF16)
    r1 = x - hi.astype(F32)
    mid = r1.astype(BF16)
    lo = (r1 - mid.astype(F32)).astype(BF16)
    return hi, mid, lo


def _exact_left(m_bf16, x):
    hi, mid, lo = _split3(x)
    dot = lambda p: jnp.dot(m_bf16, p, preferred_element_type=F32)
    return dot(hi) + dot(mid) + dot(lo)


def _exact_tn(x, m_bf16):
    hi, mid, lo = _split3(x)
    dot = lambda p: lax.dot_general(p, m_bf16, (((0,), (0,)), ((), ())), preferred_element_type=F32)
    return dot(hi) + dot(mid) + dot(lo)


def _rms(x, gain):
    return x * lax.rsqrt(jnp.mean(x * x, axis=-1, keepdims=True) + RMS_EPS) * gain


def _softplus(x):
    return jnp.maximum(x, 0.0) + jnp.log1p(jnp.exp(-jnp.abs(x)))


def _sigmoid(x):
    return 1.0 / (1.0 + jnp.exp(-x))


def _silu(x):
    return x * _sigmoid(x)


def _iota(shape, dim):
    return lax.broadcasted_iota(jnp.int32, shape, dim)


def _nilpotent_inverse(ns):
    eye = _iota((128, 128), 0) == _iota((128, 128), 1)
    plus_eye = lambda m: jnp.where(eye, 1.0, m).astype(BF16)
    inv = [plus_eye(n) for n in ns]
    nb = [n.astype(BF16) for n in ns]
    pw = [_mm(m, m) for m in nb]
    for step in range(5):
        rhs = [plus_eye(m) for m in pw]
        if step < 4:
            pwb = [m.astype(BF16) for m in pw]
            both = [_mm(jnp.concatenate([t, m], axis=0), r) for t, m, r in zip(inv, pwb, rhs)]
            inv = [b[:128].astype(BF16) for b in both]
            pw = [b[128:] - m.astype(F32) for b, m in zip(both, pwb)]
        else:
            inv = [_mm(t, r).astype(BF16) for t, r in zip(inv, rhs)]
    return inv


def _chunk_rows(j):
    return slice(j * CHUNK, (j + 1) * CHUNK)


def _is_first_step():
    return pl.program_id(1) == 0


def _is_last_step():
    return pl.program_id(1) == pl.num_programs(1) - 1


def _split_specs(npt, rows, width):
    return [pl.BlockSpec((rows, width), lambda i: (jnp.minimum(i, npt - 1), 0)),
            pl.BlockSpec((rows, width), lambda i: (jnp.maximum(i - npt, 0), 0))]


def _x_specs(x, npt):
    if len(x) == 1:
        return [pl.BlockSpec((TOKEN_TILE, D_MODEL), lambda i: (i, 0))]
    return _split_specs(npt, TOKEN_TILE, D_MODEL)


def _n_tiles(x):
    rows = sum(a.shape[0] for a in x)
    assert all(a.shape[0] % TOKEN_TILE == 0 for a in x)
    return rows // TOKEN_TILE


def _read_tile(in_prompt, refs):
    return refs[0][...] if len(refs) == 1 else jnp.where(in_prompt, refs[0][...], refs[1][...])


def _store_tile(in_prompt, p_ref, s_ref, value):
    @pl.when(in_prompt)
    def _():
        p_ref[...] = value

    @pl.when(jnp.logical_not(in_prompt))
    def _():
        s_ref[...] = value


def _proj_body(*refs, n, n_prompt_tiles, nx):
    gain_ref, w_ref, o_ref = refs[nx:]
    x = _read_tile(pl.program_id(0) < n_prompt_tiles, refs[:nx])
    h = _rms(x, gain_ref[...]).astype(BF16)
    for lo in range(0, n, PROJ_COLS):
        cols = slice(lo, min(lo + PROJ_COLS, n))
        o_ref[:, cols] = jnp.dot(h, w_ref[:, cols], preferred_element_type=F32)


def _norm_proj(x, npt, gain, w):
    n = w.shape[1]
    tm = TOKEN_TILE
    tiles = _n_tiles(x)
    return pl.pallas_call(
        functools.partial(_proj_body, n=n, n_prompt_tiles=npt, nx=len(x)),
        grid=(tiles,),
        in_specs=_x_specs(x, npt) + [pl.BlockSpec((1, D_MODEL), lambda i: (0, 0)),
                                     pl.BlockSpec((D_MODEL, n), lambda i: (0, 0))],
        out_specs=pl.BlockSpec((tm, n), lambda i: (i, 0)),
        out_shape=jax.ShapeDtypeStruct((tiles * tm, n), F32),
        compiler_params=pltpu.CompilerParams(
            dimension_semantics=("arbitrary",), vmem_limit_bytes=VMEM_LIMIT),
        name="norm_proj",
    )(*x, gain, w)


def _ffn_body(*refs, n_prompt_tiles, final, nx):
    yp_ref, ys_ref, wo_ref, gain_ref, win_ref, wout_ref, fgain_ref = refs[nx:nx + 7]
    acc_ref = refs[-1]
    in_prompt = pl.program_id(0) < n_prompt_tiles
    x = _read_tile(in_prompt, refs[:nx])
    y = _read_tile(in_prompt, (yp_ref, ys_ref))
    x1 = x + jnp.dot(y.astype(BF16), wo_ref[...], preferred_element_type=F32)
    h = _rms(x1, gain_ref[...]).astype(BF16)
    acc_ref[...] = x1
    for c in range(D_FF // FFN_COLS):
        lo = c * FFN_COLS
        g = jnp.dot(h, win_ref[:, lo:lo + FFN_COLS], preferred_element_type=F32)
        u = jnp.dot(h, win_ref[:, D_FF + lo:D_FF + lo + FFN_COLS], preferred_element_type=F32)
        a = (_silu(g) * u).astype(BF16)
        acc_ref[...] += jnp.dot(a, wout_ref[lo:lo + FFN_COLS, :], preferred_element_type=F32)
    if final:
        _store_tile(in_prompt, refs[nx + 7], refs[nx + 8], _rms(acc_ref[...], fgain_ref[...]))
    else:
        refs[nx + 7][...] = acc_ref[...]


def _out_ffn(x, y_prompt, y_sample, wo, gain, win, wout, layer, fgain, final):
    tm = TOKEN_TILE
    tiles = _n_tiles(x)
    npt = y_prompt.shape[0] // tm
    fixed = lambda i: (0, 0)
    of_layer = lambda i: (layer, 0, 0)
    if final:
        out_specs = _split_specs(npt, tm, D_MODEL)
        out_shape = [jax.ShapeDtypeStruct((y.shape[0], D_MODEL), F32) for y in (y_prompt, y_sample)]
    else:
        out_specs = [pl.BlockSpec((tm, D_MODEL), lambda i: (i, 0))]
        out_shape = [jax.ShapeDtypeStruct((tiles * tm, D_MODEL), F32)]
    return pl.pallas_call(
        functools.partial(_ffn_body, n_prompt_tiles=npt, final=final, nx=len(x)),
        grid=(tiles,),
        in_specs=_x_specs(x, npt) + _split_specs(npt, tm, D_MODEL) + [
            pl.BlockSpec((D_MODEL, D_MODEL), fixed),
            pl.BlockSpec((1, D_MODEL), fixed),
            pl.BlockSpec((None, D_MODEL, 2 * D_FF), of_layer),
            pl.BlockSpec((None, D_FF, D_MODEL), of_layer),
            pl.BlockSpec((1, D_MODEL), fixed)],
        out_specs=out_specs,
        out_shape=out_shape,
        scratch_shapes=[pltpu.VMEM((tm, D_MODEL), F32)],
        compiler_params=pltpu.CompilerParams(
            dimension_semantics=("arbitrary",), vmem_limit_bytes=VMEM_LIMIT),
        name="out_ffn",
    )(*x, y_prompt, y_sample, wo, gain, win, wout, fgain)


def _a_proj_body(*refs, tm, n_prompt_tiles, npch, cps, cpd, nx):
    (sh_ref, gain_ref, mu_ref, w0_ref, a0_ref, wr_ref, wk_ref, wv_ref, w1_ref, a1_ref, g1_ref,
     w2_ref, a2_ref, g2_ref, r_ref, k_ref, v_ref, lw_ref, al_ref, gg_ref, hl_ref) = refs[2 * nx:]
    i = pl.program_id(0)
    in_prompt = i < n_prompt_tiles
    gain = gain_ref[...]
    h = _rms(_read_tile(in_prompt, refs[:nx]), gain)
    prev_tile_last = _rms(_read_tile(in_prompt, refs[nx:2 * nx]), gain)[7:8, :]
    row = _iota((CHUNK, 1), 0)
    pieces = []
    for c in range(tm // CHUNK):
        hc = h[_chunk_rows(c)]
        natural = prev_tile_last if c == 0 else h[c * CHUNK - 1:c * CHUNK]
        gc = i * (tm // CHUNK) + c
        start = jnp.where(gc < npch, gc % cps == 0, (gc - npch) % cpd == 0)
        first = jnp.where(start, sh_ref[c:c + 1, :], natural)
        pieces.append(jnp.where(row == 0, first, pltpu.roll(hc, 1, 0)))
        hl_ref[c:c + 1, :] = hc[CHUNK - 1:CHUNK]
    xx = jnp.concatenate(pieces, axis=0) - h
    mix = lambda j: (h + xx * mu_ref[j:j + 1, :]).astype(BF16)
    dot = lambda a, w_ref: jnp.dot(a, w_ref[...], preferred_element_type=F32)
    r_ref[...] = dot(mix(0), wr_ref)
    z = w0_ref[...] + _mm(jnp.tanh(dot(mix(1), w1_ref)), w2_ref[...])
    lw_ref[...] = (-DECAY_SCALE) * _sigmoid(z)
    k_ref[...] = dot(mix(2), wk_ref)
    v_ref[...] = dot(mix(3), wv_ref)
    al_ref[...] = _sigmoid(a0_ref[...] + _mm(dot(mix(4), a1_ref), a2_ref[...]))
    gg_ref[...] = _mm(_sigmoid(dot(mix(5), g1_ref)), g2_ref[...])


def _a_proj(x, npt, shift_rows, gain, mu, w0, a0, wr, wk, wv, w1, a1, g1, w2, a2, g2, npch, cps, cpd):
    tm = TOKEN_TILE
    tiles = _n_tiles(x)
    nst = tiles - npt
    nt = tiles * tm
    row = lambda i: (i, 0)
    fixed = lambda i: (0, 0)
    full = lambda a: pl.BlockSpec(a.shape, fixed)
    tok = pl.BlockSpec((tm, D_MODEL), row)
    tok_shape = jax.ShapeDtypeStruct((nt, D_MODEL), F32)
    per8 = tm // 8
    if len(x) == 1:
        before = [pl.BlockSpec((8, D_MODEL), lambda i: (jnp.maximum(i * per8 - 1, 0), 0))]
    else:
        before = [pl.BlockSpec((8, D_MODEL), lambda i: (jnp.clip(i * per8 - 1, 0, npt * per8 - 1), 0)),
                  pl.BlockSpec((8, D_MODEL), lambda i: (jnp.clip((i - npt) * per8 - 1, 0, nst * per8 - 1), 0))]
    return pl.pallas_call(
        functools.partial(_a_proj_body, tm=tm, n_prompt_tiles=npt, npch=npch, cps=cps, cpd=cpd, nx=len(x)),
        grid=(tiles,),
        in_specs=_x_specs(x, npt) + before + [
            pl.BlockSpec((tm // CHUNK, D_MODEL), row),
            full(gain), full(mu), full(w0), full(a0), full(wr), full(wk), full(wv),
            full(w1), full(a1), full(g1), full(w2), full(a2), full(g2)],
        out_specs=[tok] * 6 + [pl.BlockSpec((tm // CHUNK, D_MODEL), row)],
        out_shape=[tok_shape] * 6 + [jax.ShapeDtypeStruct((nt // CHUNK, D_MODEL), F32)],
        compiler_params=pltpu.CompilerParams(
            dimension_semantics=("arbitrary",), vmem_limit_bytes=VMEM_LIMIT),
        name="rwkv_proj",
    )(*x, *x, shift_rows, gain, mu, w0, a0, wr, wk, wv, w1, a1, g1, w2, a2, g2)


def _a_chunk_body(r_ref, k_ref, v_ref, lw_ref, al_ref, gg_ref, kk_ref, ka_ref, rk_ref, lnw_ref, lnb_ref,
                  *rest, nchunks, carry):
    if carry:
        y_ref, sout_ref, ss_ref = rest
    else:
        sin_ref, y_ref, sout_ref = rest

    lane = _iota((1, 128), 1)
    head0 = lane < A_HEAD
    ri = _iota((128, 128), 0) % CHUNK
    ci = _iota((128, 128), 1) % CHUNK
    strict = ri > ci
    incl = ri >= ci
    tri = (_iota((CHUNK, CHUNK), 0) >= _iota((CHUNK, CHUNK), 1)).astype(BF16)
    zero = jnp.zeros((A_HEAD, A_HEAD), F32)
    same_head = (_iota((128, 128), 0) // A_HEAD) == (_iota((128, 128), 1) // A_HEAD)
    head_mean = jnp.where(same_head, 1.0 / A_HEAD, 0.0).astype(BF16)

    def headsum(x):
        s0 = jnp.sum(jnp.where(head0, x, 0.0), axis=-1, keepdims=True)
        s1 = jnp.sum(jnp.where(head0, 0.0, x), axis=-1, keepdims=True)
        return jnp.where(head0, s0, s1)

    def expand(x):
        xb = x.astype(BF16)
        zb = jnp.zeros_like(xb)
        return jnp.concatenate([jnp.where(head0, xb, zb), jnp.where(head0, zb, xb)], axis=0)

    def load_state(ref, j):
        return [jnp.concatenate([jnp.concatenate([ref[j, 2 * p], zero], axis=1),
                                 jnp.concatenate([zero, ref[j, 2 * p + 1]], axis=1)], axis=0)
                for p in range(A_PAIRS)]

    def store_state(ref, j, state):
        for p in range(A_PAIRS):
            ref[j, 2 * p] = state[p][:A_HEAD, :A_HEAD]
            ref[j, 2 * p + 1] = state[p][A_HEAD:, A_HEAD:]

    pairs = range(A_PAIRS)
    sls = [slice(p * 128, (p + 1) * 128) for p in pairs]
    units = [(j, p) for j in range(nchunks) for p in pairs]
    gcum_all = [_exact_left(tri, lw_ref[_chunk_rows(j), :]) for j in range(nchunks)]
    xs, ys, ves, ends, glasts, bonus_sums = [], [], [], [], [], []
    for j, p in units:
        rs, sl = _chunk_rows(j), sls[p]
        k = k_ref[rs, sl]
        lw = lw_ref[rs, sl]
        al = al_ref[rs, sl]
        r = r_ref[rs, sl]
        kk = k * kk_ref[:, sl]
        kk = kk * lax.rsqrt(jnp.maximum(headsum(kk * kk), 1e-24))
        kmod = k * (1.0 + (al - 1.0) * ka_ref[:, sl])
        gcum = gcum_all[j][:, sl]
        glast = gcum[CHUNK - 1:CHUNK, :]
        dec_out = jnp.exp(-gcum)
        dec_end = jnp.exp(glast - gcum)
        ae = expand(-kk * jnp.exp(gcum - lw))
        re = expand(r * jnp.exp(gcum))
        be = expand(kk * al * dec_out)
        ke = expand(kmod * dec_out)
        xs.append(jnp.concatenate([ae, re], axis=0))
        ys.append(jnp.concatenate([be, ke], axis=0))
        ves.append(expand(v_ref[rs, sl]))
        ends.append(jnp.concatenate([expand(kk * al * dec_end), expand(kmod * dec_end)], axis=0))
        glasts.append(glast)
        bonus_sums.append(headsum(r * kmod * rk_ref[:, sl]))

    sc = [_mm_nt(x, y) for x, y in zip(xs, ys)]
    n_ab = [jnp.where(strict, s[:128, :128], 0.0) for s in sc]
    l_ak = [jnp.where(strict, s[:128, 128:], 0.0).astype(BF16) for s in sc]
    m_rb = [jnp.where(incl, s[128:, :128], 0.0).astype(BF16) for s in sc]
    m_rk = [jnp.where(incl, s[128:, 128:], 0.0).astype(BF16) for s in sc]
    tinv = _nilpotent_inverse(n_ab)
    lv = [_mm(l, ve).astype(BF16) for l, ve in zip(l_ak, ves)]
    pq = [_mm(t, jnp.concatenate([x[:128], l], axis=1)).astype(BF16)
          for t, x, l in zip(tinv, xs, lv)]
    lhs = [jnp.concatenate([(x[128:] + _mm(m, z[:, :128])).astype(BF16), z[:, :128]], axis=0)
           for x, m, z in zip(xs, m_rb, pq)]
    o_intra = [_mm(jnp.concatenate([mb, mk], axis=1), jnp.concatenate([z[:, 128:], ve], axis=0))
               for mb, mk, z, ve in zip(m_rb, m_rk, pq, ves)]

    if carry:
        @pl.when(_is_first_step())
        def _():
            ss_ref[...] = jnp.zeros_like(ss_ref)

        state = [ss_ref[p] for p in pairs]
    for j in range(nchunks):
        if not carry:
            state = load_state(sin_ref, j)
        un = [j * A_PAIRS + p for p in pairs]
        ou = [_mm_nt(lhs[u], s) for u, s in zip(un, state)]
        oe = [o[:128] + o_intra[u] for o, u in zip(ou, un)]
        uv = [jnp.concatenate([(o[128:] + pq[u][:, 128:]).astype(BF16), ves[u]], axis=0) for o, u in zip(ou, un)]
        upd = [_mm_tn(a, ends[u]) for a, u in zip(uv, un)]
        state = [s * jnp.exp(glasts[u]) + d for s, u, d in zip(state, un, upd)]
        if not carry:
            store_state(sout_ref, j, state)
        rs = _chunk_rows(j)
        os = [o[:CHUNK] + o[CHUNK:] for o in oe]
        parts = []
        for o in os:
            m = jnp.concatenate([o, o * o], axis=0)
            hi = m.astype(BF16)
            parts.append(jnp.concatenate([hi, (m - hi.astype(F32)).astype(BF16)], axis=0))
        mom = [jnp.dot(m, head_mean, preferred_element_type=F32) for m in parts]
        for p, sl in enumerate(sls):
            mean = mom[p][:CHUNK] + mom[p][2 * CHUNK:3 * CHUNK]
            var = mom[p][CHUNK:2 * CHUNK] + mom[p][3 * CHUNK:] - mean * mean
            on = (os[p] - mean) * lax.rsqrt(var + A_LNX_EPS) * lnw_ref[:, sl] + lnb_ref[:, sl]
            y_ref[rs, sl] = (on + bonus_sums[un[p]] * v_ref[rs, sl]) * gg_ref[rs, sl]
    if carry:
        for p in pairs:
            ss_ref[p] = state[p]

        @pl.when(_is_last_step())
        def _():
            store_state(sout_ref, 0, [ss_ref[p] for p in pairs])


def _chunk_calls(body, name, k, prompt_in, sample_in, shared, state_in, state_shapes, y_width, scratch,
                 bp, seq, bd, dseq, layer=None):
    assert dseq == CHUNK, "sample streams are processed as single chunks"
    rows = k * CHUNK
    steps = seq // rows
    params = pltpu.CompilerParams(dimension_semantics=("arbitrary",) * 2, vmem_limit_bytes=VMEM_LIMIT)

    def run(carry, grid, tok_index, col_specs, states, out_rows, nstate, state_index):
        ndim = len(grid)
        fixed = lambda a: pl.BlockSpec(a.shape, lambda *g: (0,) * a.ndim)
        tok_specs = [pl.BlockSpec((rows, w), functools.partial(lambda c, *g: (tok_index(*g), c), c))
                     for (_, w, c) in col_specs]
        st_specs = [pl.BlockSpec((nstate,) + s, functools.partial(lambda n, *g: (state_index(*g),) + (0,) * n, len(s)))
                    for s in state_shapes]
        st_in_specs = st_specs if layer is None else [
            pl.BlockSpec((None, nstate) + s,
                         functools.partial(lambda n, *g: (layer, state_index(*g)) + (0,) * n, len(s)))
            for s in state_shapes]
        y_spec = pl.BlockSpec((rows, y_width), lambda *g: (tok_index(*g) - tok_index(*(0,) * ndim), 0))
        return pl.pallas_call(
            functools.partial(body, nchunks=k, carry=carry),
            grid=grid,
            in_specs=tok_specs + [fixed(a) for a in shared] + (st_in_specs if states else []),
            out_specs=[y_spec] + st_specs,
            out_shape=[jax.ShapeDtypeStruct((out_rows, y_width), F32)]
            + [jax.ShapeDtypeStruct((grid[0] * nstate if not carry else grid[0],) + s, F32) for s in state_shapes],
            scratch_shapes=scratch if carry else [],
            compiler_params=params if ndim == 2 else pltpu.CompilerParams(
                dimension_semantics=("arbitrary",), vmem_limit_bytes=VMEM_LIMIT),
            name=name + ("_prompt" if carry else "_sample"),
        )(*[a for (a, _, _) in col_specs], *shared, *states)

    prompt = run(True, (bp, steps), lambda b, i: b * steps + i, prompt_in, [], bp * seq, 1, lambda b, i: b)
    off = bp * seq // rows
    sample = run(False, (bd // k,), lambda i: off + i, sample_in, state_in, bd * dseq, k, lambda i: i)
    return prompt, sample


def _head_rms_gate(o, gain, gate):
    return o * lax.rsqrt(jnp.mean(o * o, axis=-1, keepdims=True) + RMS_EPS) * gain * _silu(gate)


def _b_chunk_body(q_ref, k_ref, v_ref, gate_ref, lo_ref, wa2_ref, ba_ref, onorm_ref, *rest, nchunks, carry):
    if carry:
        y_ref, sout_ref, ss_ref = rest
    else:
        sin_ref, y_ref, sout_ref = rest
    incl = _iota((CHUNK, CHUNK), 0) >= _iota((CHUNK, CHUNK), 1)
    tri = incl.astype(BF16)
    heads = range(B_HEADS)
    kss = [slice(h * B_DK, (h + 1) * B_DK) for h in heads]
    vss = [slice(h * B_DV, (h + 1) * B_DV) for h in heads]
    units = [(j, h) for j in range(nchunks) for h in heads]
    gk_all = -_softplus(-(_mm(lo_ref[...], wa2_ref[...]) + ba_ref[...])) * (1.0 / B_GATE_TAU)
    b_all = [_exact_left(tri, gk_all[_chunk_rows(j)]) for j in range(nchunks)]
    qt, kt, khat, blast = [], [], [], []
    for j, h in units:
        rs, ks = _chunk_rows(j), kss[h]
        b = b_all[j][:, ks]
        k = k_ref[rs, ks]
        blast.append(b[CHUNK - 1:CHUNK, :])
        qt.append(q_ref[rs, ks] * (B_DK ** -0.5) * jnp.exp(b))
        kt.append(k * jnp.exp(-b))
        khat.append(k * jnp.exp(blast[-1] - b))
    a = [jnp.where(incl, _mm_nt(x, y), 0.0) for x, y in zip(qt, kt)]
    o_intra = [_mm(a[u], v_ref[_chunk_rows(j), vss[h]]) for u, (j, h) in enumerate(units)]
    upd = [_mm_tn(v_ref[_chunk_rows(j), vss[h]], khat[u]) for u, (j, h) in enumerate(units)]

    if carry:
        @pl.when(_is_first_step())
        def _():
            ss_ref[...] = jnp.zeros_like(ss_ref)

        state = [ss_ref[h] for h in heads]
    for j in range(nchunks):
        if not carry:
            state = [sin_ref[j, h].T for h in heads]
        rs = _chunk_rows(j)
        o = [o_intra[j * B_HEADS + h] + _mm_nt(qt[j * B_HEADS + h], state[h]) for h in heads]
        state = [state[h] * jnp.exp(blast[j * B_HEADS + h]) + upd[j * B_HEADS + h] for h in heads]
        for h in heads:
            if not carry:
                sout_ref[j, h] = state[h].T
            y_ref[rs, vss[h]] = _head_rms_gate(o[h], onorm_ref[...], gate_ref[rs, vss[h]])
    if carry:
        for h in heads:
            ss_ref[h] = state[h]

        @pl.when(_is_last_step())
        def _():
            for h in heads:
                sout_ref[0, h] = ss_ref[h].T


def _c_chunk_body(qkv_ref, z_ref, ba_ref, cw_ref, alog_ref, dtb_ref, onorm_ref, *rest, nchunks, carry):
    if carry:
        y_ref, cvout_ref, sout_ref, pv_ref, ss_ref = rest
    else:
        cvin_ref, sin_ref, y_ref, cvout_ref, sout_ref = rest

    def conv_act(x, pv):
        row8 = _iota((8, 1), 0)
        conv = x * cw_ref[C_CONV - 1:C_CONV, :]
        for j in range(1, C_CONV):
            sh = pltpu.roll(x, j, 0)
            head = jnp.where(row8 < j, pltpu.roll(pv, j, 0), sh[:8])
            sh = jnp.concatenate([head, sh[8:]], axis=0)
            conv = conv + sh * cw_ref[C_CONV - 1 - j:C_CONV - j, :]
        return _silu(conv)

    if carry:
        @pl.when(_is_first_step())
        def _():
            ss_ref[...] = jnp.zeros_like(ss_ref)
            pv_ref[...] = jnp.zeros_like(pv_ref)

        x = qkv_ref[...]
        act = conv_act(x, pv_ref[...])
        pv_ref[...] = x[nchunks * CHUNK - 8:]
    else:
        acts = []
        for j in range(nchunks):
            x = qkv_ref[_chunk_rows(j), :]
            acts.append(conv_act(x, cvin_ref[j]))
            cvout_ref[j] = x[CHUNK - 8:]
        act = jnp.concatenate(acts, axis=0)

    ba = ba_ref[...]
    beta_all = _sigmoid(ba)
    g_all = -jnp.exp(alog_ref[...]) * _softplus(ba + dtb_ref[...])
    tri = (_iota((CHUNK, CHUNK), 0) >= _iota((CHUNK, CHUNK), 1)).astype(BF16)
    triu2 = (_iota((CHUNK, 128), 0) <= _iota((CHUNK, 128), 1) % CHUNK).astype(BF16)
    ri = _iota((128, 128), 0)
    ci = _iota((128, 128), 1)
    same = (ri // CHUNK) == (ci // CHUNK)
    strict = same & (ri % CHUNK > ci % CHUNK)
    incl = same & (ri % CHUNK >= ci % CHUNK)
    lane_lo = _iota((1, 128), 1) < CHUNK
    zeros = jnp.zeros((CHUNK, C_DK), BF16)
    l2 = lambda t: t * lax.rsqrt(jnp.sum(t * t, axis=-1, keepdims=True) + 1e-6)
    blockdiag = lambda a, b: jnp.concatenate(
        [jnp.concatenate([a.astype(BF16), zeros], axis=1),
         jnp.concatenate([zeros, b.astype(BF16)], axis=1)], axis=0)

    heads = range(C_HEADS)
    npairs = C_HEADS // 2
    q, k, v, beta, kb, col, egc, glast = ({} for _ in range(8))
    gc_row = []
    for j in range(nchunks):
        rs = _chunk_rows(j)
        gc_col = _exact_left(tri, g_all[rs])
        gc_row.append(_exact_tn(g_all[rs], triu2))
        for h in heads:
            u = (j, h)
            q[u] = l2(act[rs, h * C_DK:(h + 1) * C_DK]) * (C_DK ** -0.5)
            k[u] = l2(act[rs, C_HEADS * C_DK + h * C_DK:C_HEADS * C_DK + (h + 1) * C_DK])
            v[u] = act[rs, 2 * C_HEADS * C_DK + h * C_DV:2 * C_HEADS * C_DK + (h + 1) * C_DV]
            beta[u] = beta_all[rs, h:h + 1]
            kb[u] = k[u] * beta[u]
            col[u] = gc_col[:, C_HEADS + h:C_HEADS + h + 1]
            egc[u] = jnp.exp(col[u])
            glast[u] = col[u][CHUNK - 1:CHUNK, :]
    units = [(j, p) for j in range(nchunks) for p in range(npairs)]
    decay, sc, rhs = [], [], []
    for j, p in units:
        a, b = (j, 2 * p), (j, 2 * p + 1)
        col_pair = jnp.concatenate([col[a], col[b]], axis=0)
        row_pair = jnp.where(lane_lo, gc_row[j][C_HEADS + 2 * p:C_HEADS + 2 * p + 1, :],
                             gc_row[j][C_HEADS + 2 * p + 1:C_HEADS + 2 * p + 2, :])
        decay.append(jnp.where(incl, jnp.exp(jnp.where(incl, col_pair - row_pair, 0.0)), 0.0))
        lhs = jnp.concatenate([blockdiag(kb[a], kb[b]), blockdiag(q[a], q[b])], axis=0)
        sc.append(_mm_nt(lhs, blockdiag(k[a], k[b])))
        rhs.append(jnp.concatenate(
            [jnp.concatenate([v[u] * beta[u], kb[u] * egc[u]], axis=1) for u in (a, b)], axis=0).astype(BF16))
    n_l = [-jnp.where(strict, s[:128], 0.0) * d for s, d in zip(sc, decay)]
    attn = [(s[128:] * d).astype(BF16) for s, d in zip(sc, decay)]
    tinv = _nilpotent_inverse(n_l)
    sol = [_mm(t, r) for t, r in zip(tinv, rhs)]

    if carry:
        state = [ss_ref[h] for h in heads]
    for j in range(nchunks):
        if not carry:
            state = [sin_ref[j, h] for h in heads]
        rs = _chunk_rows(j)
        half = lambda m, h: m[(h % 2) * CHUNK:(h % 2 + 1) * CHUNK]
        qw = [_mm(jnp.concatenate([q[j, h] * egc[j, h], half(sol[j * npairs + h // 2], h)[:, C_DV:]], axis=0),
                  state[h]) for h in heads]
        v_new = [(sol[j * npairs + p][:, :C_DV]
                  - jnp.concatenate([qw[2 * p][CHUNK:], qw[2 * p + 1][CHUNK:]], axis=0)).astype(BF16)
                 for p in range(npairs)]
        o_intra = [_mm(attn[j * npairs + p], v_new[p]) for p in range(npairs)]
        upd = [_mm_tn(k[j, h] * jnp.exp(glast[j, h] - col[j, h]), half(v_new[h // 2], h)) for h in heads]
        state = [state[h] * jnp.exp(glast[j, h]) + upd[h] for h in heads]
        for h in heads:
            o = qw[h][:CHUNK] + half(o_intra[h // 2], h)
            vs = slice(h * C_DV, (h + 1) * C_DV)
            y_ref[rs, vs] = _head_rms_gate(o, onorm_ref[...], z_ref[rs, vs])
            if not carry:
                sout_ref[j, h] = state[h]
    if carry:
        for h in heads:
            ss_ref[h] = state[h]

        @pl.when(_is_last_step())
        def _():
            sout_ref[0] = ss_ref[...]
            cvout_ref[0] = pv_ref[...]


def _pad_cols(w, n):
    return jnp.pad(w, ((0, 0), (0, n - w.shape[1])))


def kernel(x_prompt, x_sample, state_a_shift, state_a_wkv, state_b_kv, state_c_conv, state_c_kv, norm_mix, norm_ffn, norm_final, ffn_w_in, ffn_w_out, a_mu, a_w0, a_w1, a_w2, a_a0, a_a1, a_a2, a_g1, a_g2, a_k_k, a_k_a, a_r_k, a_w_rkv, a_w_o, a_lnx_w, a_lnx_b, b_w_in, b_w_a1, b_w_a2, b_b_a, b_onorm, b_w_o, c_w_in, c_conv_w, c_a_log, c_dt_bias, c_onorm, c_w_o):
    bp, seq, d = x_prompt.shape
    bd, dseq, _ = x_sample.shape
    depth = norm_mix.shape[0]
    cps, cpd = seq // CHUNK, dseq // CHUNK
    npch = bp * cps
    shape = (bp, seq, bd, dseq)
    x = (x_prompt.reshape(bp * seq, d), x_sample.reshape(bd * dseq, d))
    npt = bp * seq // TOKEN_TILE
    n_chunks = (bp * seq + bd * dseq) // CHUNK
    row = lambda a: a.reshape(1, -1)
    bf = lambda a: a.astype(BF16)
    ffn_in, ffn_out = bf(ffn_w_in), bf(ffn_w_out)

    last_prompt = [(s + 1) * cps - 1 for s in range(bp)]
    last_sample = [npch + (s + 1) * cpd - 1 for s in range(bd)]

    out_a_shift, out_a_wkv, out_b_kv, out_c_conv, out_c_kv = [], [], [], [], []
    for i in range(depth):
        j = i // 3
        gain = row(norm_mix[i])
        if i % 3 == 0:
            shift_rows = jnp.zeros((n_chunks, d), F32)
            shift_rows = shift_rows.at[jnp.asarray([npch + s * cpd for s in range(bd)])].set(state_a_shift[j])
            proj = _a_proj(
                x, npt, shift_rows, gain, a_mu[j], row(a_w0[j]), row(a_a0[j]),
                bf(a_w_rkv[j, 0]), bf(a_w_rkv[j, 1]), bf(a_w_rkv[j, 2]),
                bf(a_w1[j]), bf(a_a1[j]), bf(a_g1[j]), bf(a_w2[j]), bf(a_a2[j]), bf(a_g2[j]), npch, cps, cpd)
            hl = proj[6]
            cols = [(a, D_MODEL, 0) for a in proj[:6]]
            shared = [row(a_k_k[j]), row(a_k_a[j]), row(a_r_k[j]), row(a_lnx_w[j]), row(a_lnx_b[j])]
            (y_p, wkv_p), (y_s, wkv_s) = _chunk_calls(
                _a_chunk_body, "rwkv_chunk", A_CHUNKS_PER_STEP, cols, cols, shared, [state_a_wkv],
                [(A_HEADS, A_HEAD, A_HEAD)], D_MODEL, [pltpu.VMEM((A_PAIRS, 128, 128), F32)], *shape, layer=j)
            out_a_shift.append((hl[jnp.asarray(last_prompt)], hl[jnp.asarray(last_sample)]))
            out_a_wkv.append((wkv_p, wkv_s))
            w_o = a_w_o[j]
        elif i % 3 == 1:
            w = jnp.concatenate([b_w_in[j], _pad_cols(b_w_a1[j], B_LORA_PAD)], axis=1)
            proj = _norm_proj(x, npt, gain, bf(w))
            wa2 = jnp.pad(b_w_a2[j], ((0, B_LORA_PAD - b_w_a2.shape[1]), (0, 0)))
            qk_w, v_w = B_HEADS * B_DK, B_HEADS * B_DV
            cols = [(proj, qk_w, 0), (proj, qk_w, 1), (proj, v_w, 1), (proj, v_w, 2),
                    (proj, B_LORA_PAD, (2 * qk_w + 2 * v_w) // B_LORA_PAD)]
            shared = [bf(wa2), row(b_b_a[j]), row(b_onorm[j])]
            (y_p, kv_p), (y_s, kv_s) = _chunk_calls(
                _b_chunk_body, "gla_chunk", B_CHUNKS_PER_STEP, cols, cols, shared, [state_b_kv[j]],
                [(B_HEADS, B_DK, B_DV)], v_w, [pltpu.VMEM((B_HEADS, B_DV, B_DK), F32)], *shape)
            out_b_kv.append((kv_p, kv_s))
            w_o = b_w_o[j]
        else:
            proj = _norm_proj(x, npt, gain, bf(_pad_cols(c_w_in[j], C_PROJ_PAD)))
            lanes = lambda a: jnp.pad(a.reshape(1, -1), ((0, 0), (C_HEADS, 128 - 2 * C_HEADS)))
            z_w = C_HEADS * C_DV
            cols = [(proj, C_QKV, 0), (proj, z_w, C_QKV // z_w), (proj, 128, (C_QKV + z_w) // 128)]
            shared = [c_conv_w[j], lanes(c_a_log[j]), lanes(c_dt_bias[j]), row(c_onorm[j])]
            cv_in = jnp.pad(state_c_conv[j], ((0, 0), (8 - (C_CONV - 1), 0), (0, 0)))
            (y_p, cv_p, kv_p), (y_s, cv_s, kv_s) = _chunk_calls(
                _c_chunk_body, "gdn_chunk", C_CHUNKS_PER_STEP, cols, cols, shared, [cv_in, state_c_kv[j]],
                [(8, C_QKV), (C_HEADS, C_DK, C_DV)], z_w,
                [pltpu.VMEM((8, C_QKV), F32), pltpu.VMEM((C_HEADS, C_DK, C_DV), F32)], *shape)
            out_c_conv.append((cv_p[:, 8 - (C_CONV - 1):], cv_s[:, 8 - (C_CONV - 1):]))
            out_c_kv.append((kv_p, kv_s))
            w_o = c_w_o[j]
        final = i == depth - 1
        x = tuple(_out_ffn(x, y_p, y_s, bf(w_o), row(norm_ffn[i]), ffn_in, ffn_out, i,
                           row(norm_final), final))

    split = lambda parts: (jnp.stack([p for p, _ in parts]), jnp.stack([s for _, s in parts]))
    a_shift_p, a_shift_s = split(out_a_shift)
    a_wkv_p, a_wkv_s = split(out_a_wkv)
    b_kv_p, b_kv_s = split(out_b_kv)
    c_conv_p, c_conv_s = split(out_c_conv)
    c_kv_p, c_kv_s = split(out_c_kv)
    y_prompt = x[0].reshape(bp, seq, d)
    y_sample = x[1].reshape(bd, dseq, d)
    return (y_prompt, y_sample, a_shift_p, a_wkv_p, b_kv_p, c_conv_p, c_kv_p,
            a_shift_s, a_wkv_s, b_kv_s, c_conv_s, c_kv_s)
```
